```python
import jax, jax.numpy as jnp
from jax import lax
import numpy as np

D_MODEL = 1024
BATCH = 8
SEQ = 8192
DEPTH = 2
DEC_BATCH = 16
DEC_SEQ = 2048
PAST_LEN = 128

NORM_EPS = 1e-6
RET_HEADS = 4
RET_QK_DIM = 64
RET_V_DIM = 128
RET_CHUNK = 128
RET_THETA = 10000.0
ATT_PATTERNS = ((128, 1), (512, 4), (2048, 16))
ATT_HEADS = 4
ATT_HEAD_DIM = 128
ATT_ROT_DIM = ATT_HEAD_DIM // 4
ROPE_THETA = 500000.0
SG_GROUPS = 4
SG_CHUNK = 128
SG_WIDTH = 512
N_BRANCH = 3
N_GROUPS = 4
EXP_PER_GROUP = 8
N_EXPERTS = N_GROUPS * EXP_PER_GROUP
TOP_K = 2
EXP_HIDDEN = 512
MOE_BLOCK = 256
RET_QK_W = RET_HEADS * RET_QK_DIM
RET_V_W = RET_HEADS * RET_V_DIM
ATT_W = ATT_HEADS * ATT_HEAD_DIM
ATT_IN_W = len(ATT_PATTERNS) * 3 * ATT_W
SG_IN_W = 2 * SG_WIDTH
GATE_IN_W = N_BRANCH * D_MODEL
N_IN = 2 * RET_QK_W + 2 * RET_V_W + ATT_IN_W + SG_IN_W + GATE_IN_W
NEG_BIG = -1e30

kernel_name = 'hybrid_retention_dilated_sgu_hmoe_encoder'


def rms_norm(x, gain):
    x32 = x.astype(jnp.float32)
    y = x32 * lax.rsqrt(jnp.mean(x32 * x32, axis=-1, keepdims=True) + NORM_EPS)
    return (y * gain.astype(jnp.float32)).astype(x.dtype)


def layer_norm(x, gain):
    x32 = x.astype(jnp.float32)
    mu = jnp.mean(x32, axis=-1, keepdims=True)
    var = jnp.mean(jnp.square(x32 - mu), axis=-1, keepdims=True)
    return ((x32 - mu) * lax.rsqrt(var + NORM_EPS) * gain.astype(jnp.float32)).astype(x.dtype)


def head_group_norm(o, gain):
    B, S, H, dv = o.shape
    o32 = o.astype(jnp.float32)
    mu = jnp.mean(o32, axis=-1, keepdims=True)
    var = jnp.mean(jnp.square(o32 - mu), axis=-1, keepdims=True)
    y = ((o32 - mu) * lax.rsqrt(var + NORM_EPS)).reshape(B, S, H * dv) * gain.astype(jnp.float32)
    return y.astype(o.dtype)


def rotary(x, n_rot, theta):
    S = x.shape[1]
    half = n_rot // 2
    inv = jnp.power(jnp.float32(theta), -jnp.arange(half, dtype=jnp.float32) * (2.0 / n_rot))
    ang = jnp.arange(S, dtype=jnp.float32)[:, None] * inv[None, :]
    cos = jnp.cos(ang)[None, :, None, :].astype(x.dtype)
    sin = jnp.sin(ang)[None, :, None, :].astype(x.dtype)
    x1 = x[..., :half]
    x2 = x[..., half:n_rot]
    return jnp.concatenate([x1 * cos - x2 * sin, x1 * sin + x2 * cos, x[..., n_rot:]], axis=-1)


def chunk_retention(q, k, v, log_g, include_diag):
    B, S, H, dk = q.shape
    dv = v.shape[-1]
    C = RET_CHUNK
    nc = S // C
    dt = q.dtype
    qc = q.reshape(B, nc, C, H, dk)
    kc = k.reshape(B, nc, C, H, dk)
    vc = v.reshape(B, nc, C, H, dv)
    pos = jnp.arange(C, dtype=jnp.float32)
    diff = pos[:, None] - pos[None, :]
    mask = (diff >= 0) if include_diag else (diff > 0)
    lg = log_g[:, None, None]
    dec = jnp.where(mask, jnp.exp(jnp.where(mask, diff, 0.0) * lg), 0.0).astype(dt)
    s = jnp.einsum('bcnhk,bcmhk->bchnm', qc, kc) * dec[None, None]
    inner = jnp.einsum('bchnm,bcmhv->bcnhv', s, vc)
    k_dec = jnp.exp((C - 1 - pos)[None, :] * log_g[:, None]).astype(dt)
    kv = jnp.einsum('bcmhk,hm,bcmhv->cbhkv', kc, k_dec, vc)
    chunk_dec = jnp.exp(C * log_g).astype(dt)[None, :, None, None]

    def step(state, kv_c):
        return chunk_dec * state + kv_c, state

    _, prev = lax.scan(step, jnp.zeros((B, H, dk, dv), dt), kv)
    q_dec = jnp.exp((pos + 1.0)[None, :] * log_g[:, None]).astype(dt)
    cross = jnp.einsum('bcnhk,hn,cbhkv->bcnhv', qc, q_dec, prev)
    return (inner + cross).reshape(B, S, H, dv)


def dilated_window_attention(q, k, v, window, dilation):
    B, S, H, hd = q.shape
    R = window // (2 * dilation)
    L = S // dilation
    nb = -(-L // R)
    Lp = nb * R

    def fold(t):
        return t.reshape(B, L, dilation, H, hd).transpose(0, 2, 3, 1, 4)

    qf = jnp.pad(fold(q), ((0, 0), (0, 0), (0, 0), (0, Lp - L), (0, 0))).reshape(B, dilation, H, nb, R, hd)

    def key_windows(t):
        tp = jnp.pad(fold(t), ((0, 0), (0, 0), (0, 0), (R, Lp - L + R), (0, 0))).reshape(B, dilation, H, nb + 2, R, hd)
        return jnp.concatenate([tp[:, :, :, :-2], tp[:, :, :, 1:-1], tp[:, :, :, 2:]], axis=4)

    kw = key_windows(k)
    vw = key_windows(v)
    s = jnp.einsum('bdhnqe,bdhnke->bdhnqk', qf, kw).astype(jnp.float32) * (hd ** -0.5)
    t = jnp.arange(R)[:, None]
    j = jnp.arange(3 * R)[None, :]
    rel = j - R - t
    kpos = jnp.arange(nb)[:, None, None] * R + (j - R)[None]
    valid = (jnp.abs(rel) <= R)[None] & (kpos >= 0) & (kpos < L)
    s = jnp.where(valid, s, NEG_BIG)
    m = jnp.max(s, axis=-1, keepdims=True)
    p = jnp.exp(s - m)
    den = jnp.sum(p, axis=-1, keepdims=True)
    o = jnp.einsum('bdhnqk,bdhnke->bdhnqe', (p / den).astype(v.dtype), vw)
    lse = (m + jnp.log(den))[..., 0]
    o = o.reshape(B, dilation, H, Lp, hd)[:, :, :, :L].transpose(0, 3, 1, 2, 4).reshape(B, S, H, hd)
    lse = lse.reshape(B, dilation, H, Lp)[..., :L].transpose(0, 3, 1, 2).reshape(B, S, H)
    return o, lse


def token_mixer(h, w_in, dec_f, dec_b, gn_gain, sg_ln_gain, sg_w, sg_b, w_ret_out, w_att_out, w_sg_out, w_o):
    B, S, _ = h.shape
    dt = h.dtype
    proj = h @ w_in
    sizes = (RET_QK_W, RET_QK_W, RET_V_W, RET_V_W, ATT_IN_W, SG_IN_W)
    rq, rk, rv, rg, att, sg, gate = jnp.split(proj, np.cumsum(sizes), axis=-1)

    rq = rotary(rq.reshape(B, S, RET_HEADS, RET_QK_DIM), RET_QK_DIM, RET_THETA)
    rk = rotary(rk.reshape(B, S, RET_HEADS, RET_QK_DIM), RET_QK_DIM, RET_THETA) * (RET_QK_DIM ** -0.5)
    rv = rv.reshape(B, S, RET_HEADS, RET_V_DIM)
    lg_f = -jnp.exp(dec_f.astype(jnp.float32))
    lg_b = -jnp.exp(dec_b.astype(jnp.float32))
    ret_fwd = chunk_retention(rq, rk, rv, lg_f, True)
    ret_bwd = chunk_retention(rq[:, ::-1], rk[:, ::-1], rv[:, ::-1], lg_b, False)[:, ::-1]
    r = head_group_norm(ret_fwd + ret_bwd, gn_gain) * jax.nn.silu(rg)
    y_ret = r @ w_ret_out

    att = att.reshape(B, S, len(ATT_PATTERNS), 3, ATT_HEADS, ATT_HEAD_DIM)
    outs = []
    lses = []
    for g, (window, dil) in enumerate(ATT_PATTERNS):
        q = rotary(att[:, :, g, 0], ATT_ROT_DIM, ROPE_THETA)
        k = rotary(att[:, :, g, 1], ATT_ROT_DIM, ROPE_THETA)
        o, lse = dilated_window_attention(q, k, att[:, :, g, 2], window, dil)
        outs.append(o)
        lses.append(lse)
    wts = jax.nn.softmax(jnp.stack(lses, axis=0), axis=0).astype(dt)
    a = jnp.sum(wts[..., None] * jnp.stack(outs, axis=0), axis=0).reshape(B, S, ATT_W)
    y_att = a @ w_att_out

    sg = jax.nn.gelu(sg)
    u, v = jnp.split(sg, 2, axis=-1)
    v = layer_norm(v, sg_ln_gain)
    vc = v.reshape(B, S // SG_CHUNK, SG_CHUNK, SG_GROUPS, SG_WIDTH // SG_GROUPS)
    vs = jnp.einsum('gnm,bcmgf->bcngf', sg_w, vc) + sg_b.T[None, None, :, :, None]
    y_sg = (u * vs.reshape(B, S, SG_WIDTH)) @ w_sg_out

    gates = jax.nn.sigmoid(gate.reshape(B, S, N_BRANCH, D_MODEL))
    merged = gates[:, :, 0] * y_ret + gates[:, :, 1] * y_att + gates[:, :, 2] * y_sg
    return merged @ w_o


def hier_moe(x, w_rg, b_rg, w_re, b_re, w_gate, w_up, w_down):
    B, S, D = x.shape
    T = B * S
    xt = x.reshape(T, D)
    g_logits = (xt @ w_rg).astype(jnp.float32) + b_rg.astype(jnp.float32)
    g_prob = jax.nn.softmax(g_logits, axis=-1)
    g_idx = jnp.argmax(g_logits, axis=-1).astype(jnp.int32)
    g_w = jnp.take_along_axis(g_prob, g_idx[:, None], axis=-1)[:, 0]
    e_logits = ((xt @ w_re).astype(jnp.float32) + b_re.astype(jnp.float32)).reshape(T, N_GROUPS, EXP_PER_GROUP)
    e_logits = jnp.take_along_axis(e_logits, g_idx[:, None, None], axis=1)[:, 0]
    top_v, top_i = lax.top_k(e_logits, TOP_K)
    top_w = jax.nn.softmax(top_v, axis=-1) * g_w[:, None]
    eid = (g_idx[:, None] * EXP_PER_GROUP + top_i.astype(jnp.int32)).reshape(-1)
    tok = jnp.repeat(jnp.arange(T, dtype=jnp.int32), TOP_K)
    wts = top_w.reshape(-1)
    A = T * TOP_K
    order = jnp.argsort(eid)
    s_eid = eid[order]
    s_tok = tok[order]
    s_w = wts[order]
    counts = jnp.zeros((N_EXPERTS,), jnp.int32).at[eid].add(1)
    start = jnp.cumsum(counts) - counts
    padded = (counts + MOE_BLOCK - 1) // MOE_BLOCK * MOE_BLOCK
    pad_end = jnp.cumsum(padded)
    pad_start = pad_end - padded
    dest = pad_start[s_eid] + (jnp.arange(A, dtype=jnp.int32) - start[s_eid])
    P = A + N_EXPERTS * MOE_BLOCK
    nblk = P // MOE_BLOCK
    buf_tok = jnp.full((P,), T, jnp.int32).at[dest].set(s_tok)
    buf_w = jnp.zeros((P,), jnp.float32).at[dest].set(s_w)
    blk_e = jnp.minimum(jnp.searchsorted(pad_end, jnp.arange(nblk, dtype=jnp.int32) * MOE_BLOCK, side='right'), N_EXPERTS - 1)
    x_pad = jnp.concatenate([xt, jnp.zeros((1, D), xt.dtype)], axis=0)
    xb = x_pad[buf_tok].reshape(nblk, MOE_BLOCK, D)

    def expert_block(args):
        xblk, e = args
        hid = jax.nn.silu(xblk @ w_gate[e]) * (xblk @ w_up[e])
        return hid @ w_down[e]

    yb = lax.map(expert_block, (xb, blk_e)).reshape(P, D)
    y = jnp.zeros((T + 1, D), x.dtype).at[buf_tok].add(yb * buf_w[:, None].astype(x.dtype))
    return y[:T].reshape(B, S, D)


def run_trunk(x, norm_mix, w_in, ret_decay_fwd, ret_decay_bwd, ret_gn_gain, sg_ln_gain, sg_w, sg_b,
              w_ret_out, w_att_out, w_sg_out, w_o, norm_ffn, w_router_group, b_router_group,
              w_router_expert, b_router_expert, w_exp_gate, w_exp_up, w_exp_down, norm_final):
    for l in range(DEPTH):
        h = rms_norm(x, norm_mix[l])
        x = x + token_mixer(h, w_in[l], ret_decay_fwd[l], ret_decay_bwd[l], ret_gn_gain[l], sg_ln_gain[l],
                            sg_w[l], sg_b[l], w_ret_out[l], w_att_out[l], w_sg_out[l], w_o[l])
        h = rms_norm(x, norm_ffn[l])
        x = x + hier_moe(h, w_router_group[l], b_router_group[l], w_router_expert[l], b_router_expert[l],
                         w_exp_gate[l], w_exp_up[l], w_exp_down[l])
    return rms_norm(x, norm_final)


def setup_inputs(seed: int = 0) -> dict:
    key = jax.random.key(seed)
    ks = jax.random.split(key, 24)
    f32 = jnp.float32

    def nrm(k, shape, scale):
        return jax.random.normal(k, shape, f32) * scale

    def gain(k, shape):
        return 1.0 + 0.05 * jax.random.normal(k, shape, f32)

    base = jnp.log(-jnp.log1p(-jnp.power(2.0, -5.0 - jnp.arange(RET_HEADS, dtype=f32))))
    return {
        'x_prompt': nrm(ks[0], (BATCH, SEQ, D_MODEL), 1.0),
        'x_sample': nrm(ks[1], (DEC_BATCH, DEC_SEQ, D_MODEL), 1.0),
        'norm_mix': gain(ks[2], (DEPTH, D_MODEL)),
        'w_in': nrm(ks[3], (DEPTH, D_MODEL, N_IN), D_MODEL ** -0.5),
        'ret_decay_fwd': base[None, :] + nrm(ks[4], (DEPTH, RET_HEADS), 0.1),
        'ret_decay_bwd': base[None, :] + nrm(ks[5], (DEPTH, RET_HEADS), 0.1),
        'ret_gn_gain': gain(ks[6], (DEPTH, RET_V_W)),
        'sg_ln_gain': gain(ks[7], (DEPTH, SG_WIDTH)),
        'sg_w': nrm(ks[8], (DEPTH, SG_GROUPS, SG_CHUNK, SG_CHUNK), SG_CHUNK ** -0.5),
        'sg_b': 1.0 + nrm(ks[9], (DEPTH, SG_GROUPS, SG_CHUNK), 0.1),
        'w_ret_out': nrm(ks[10], (DEPTH, RET_V_W, D_MODEL), RET_V_W ** -0.5),
        'w_att_out': nrm(ks[11], (DEPTH, ATT_W, D_MODEL), ATT_W ** -0.5),
        'w_sg_out': nrm(ks[12], (DEPTH, SG_WIDTH, D_MODEL), SG_WIDTH ** -0.5),
        'w_o': nrm(ks[13], (DEPTH, D_MODEL, D_MODEL), D_MODEL ** -0.5),
        'norm_ffn': gain(ks[14], (DEPTH, D_MODEL)),
        'w_router_group': nrm(ks[15], (DEPTH, D_MODEL, N_GROUPS), D_MODEL ** -0.5),
        'b_router_group': nrm(ks[16], (DEPTH, N_GROUPS), 0.01),
        'w_router_expert': nrm(ks[17], (DEPTH, D_MODEL, N_EXPERTS), D_MODEL ** -0.5),
        'b_router_expert': nrm(ks[18], (DEPTH, N_EXPERTS), 0.01),
        'w_exp_gate': nrm(ks[19], (DEPTH, N_EXPERTS, D_MODEL, EXP_HIDDEN), D_MODEL ** -0.5),
        'w_exp_up': nrm(ks[20], (DEPTH, N_EXPERTS, D_MODEL, EXP_HIDDEN), D_MODEL ** -0.5),
        'w_exp_down': nrm(ks[21], (DEPTH, N_EXPERTS, EXP_HIDDEN, D_MODEL), EXP_HIDDEN ** -0.5),
        'norm_final': gain(ks[22], (D_MODEL,)),
    }


def reference(x_prompt, x_sample, norm_mix, w_in, ret_decay_fwd, ret_decay_bwd, ret_gn_gain, sg_ln_gain,
              sg_w, sg_b, w_ret_out, w_att_out, w_sg_out, w_o, norm_ffn, w_router_group, b_router_group,
              w_router_expert, b_router_expert, w_exp_gate, w_exp_up, w_exp_down, norm_final):
    params = (norm_mix, w_in, ret_decay_fwd, ret_decay_bwd, ret_gn_gain, sg_ln_gain, sg_w, sg_b,
              w_ret_out, w_att_out, w_sg_out, w_o, norm_ffn, w_router_group, b_router_group,
              w_router_expert, b_router_expert, w_exp_gate, w_exp_up, w_exp_down, norm_final)
    y_prompt = run_trunk(x_prompt, *params)
    y_sample = run_trunk(x_sample, *params)
    return (y_prompt, y_sample)
```

```python
import functools

import jax
import jax.numpy as jnp
import numpy as np
from jax import lax
from jax.experimental import pallas as pl
from jax.experimental.pallas import tpu as pltpu

F32 = jnp.float32
BF16 = jnp.bfloat16
U32 = jnp.uint32
I32 = jnp.int32

D_MODEL = 1024
DEPTH = 2
NORM_EPS = 1e-6
RET_HEADS = 4
RET_QK_DIM = 64
RET_V_DIM = 128
RET_CHUNK = 128
RET_THETA = 10000.0
ATT_PATTERNS = ((128, 1), (512, 4), (2048, 16))
ATT_HEADS = 4
ATT_HEAD_DIM = 128
ATT_ROT_DIM = ATT_HEAD_DIM // 4
ROPE_THETA = 500000.0
SG_GROUPS = 4
SG_CHUNK = 128
SG_WIDTH = 512
N_BRANCH = 3
N_GROUPS = 4
EXP_PER_GROUP = 8
N_EXPERTS = N_GROUPS * EXP_PER_GROUP
TOP_K = 2
EXP_HIDDEN = 512
RET_QK_W = RET_HEADS * RET_QK_DIM
RET_V_W = RET_HEADS * RET_V_DIM
ATT_W = ATT_HEADS * ATT_HEAD_DIM
ATT_IN_W = len(ATT_PATTERNS) * 3 * ATT_W
SG_IN_W = 2 * SG_WIDTH
GATE_IN_W = N_BRANCH * D_MODEL
N_IN = 2 * RET_QK_W + 2 * RET_V_W + ATT_IN_W + SG_IN_W + GATE_IN_W
NEG_BIG = -1e30

LANES = 128
COL_BLK = 512
N_COL_BLK = N_IN // COL_BLK
ATT_R = 64
assert all(w // (2 * d) == ATT_R for w, d in ATT_PATTERNS)
CB_RQK, CB_RV, CB_RG, CB_ATT, CB_SGU, CB_SGV, CB_GATE = 0, 1, 2, 3, 12, 13, 14
ROUTE_LANE0 = N_GROUPS
MOE_BLOCK = 256
VMEM_LIMIT = 56 * 1024 * 1024


def _cparams(sem, vmem=VMEM_LIMIT):
    return pltpu.CompilerParams(dimension_semantics=sem, vmem_limit_bytes=vmem)


def _sigmoid(x):
    return 1.0 / (1.0 + jnp.exp(-x))


def _in_proj_kernel(x_ref, g_ref, w_ref, rc_ref, rs_ref, ac_ref, as_ref, o_ref, h_ref):
    j = pl.program_id(1)

    @pl.when(j == 0)
    def _():
        x = x_ref[...]
        ms = jnp.mean(x * x, axis=-1, keepdims=True)
        h_ref[...] = (x * lax.rsqrt(ms + NORM_EPS) * g_ref[...]).astype(BF16)

    acc = jnp.dot(h_ref[...], w_ref[...], preferred_element_type=F32)
    lane = lax.broadcasted_iota(I32, acc.shape, 1)
    reps = COL_BLK // LANES

    def rotate(cos_ref, sin_ref, half, width):
        cos = jnp.concatenate([cos_ref[...]] * reps, axis=1)
        sin = jnp.concatenate([sin_ref[...]] * reps, axis=1)
        from_lo = pltpu.roll(acc, half, 1)
        from_hi = pltpu.roll(acc, COL_BLK - half, 1)
        partner = jnp.where((lane & (width - 1)) < half, from_hi, from_lo)
        return acc * cos + partner * sin

    att_rel = j - CB_ATT
    is_att = jnp.logical_and(j >= CB_ATT, j < CB_SGU)
    is_att_qk = jnp.logical_and(is_att, att_rel % 3 != 2)
    is_plain = jnp.logical_or(j == CB_RV, jnp.logical_and(is_att, att_rel % 3 == 2))

    @pl.when(j == CB_RQK)
    def _():
        y = rotate(rc_ref, rs_ref, RET_QK_DIM // 2, RET_QK_DIM)
        y = jnp.where(lane >= RET_QK_W, y * (RET_QK_DIM ** -0.5), y)
        o_ref[...] = y.astype(BF16)

    @pl.when(is_att_qk)
    def _():
        o_ref[...] = rotate(ac_ref, as_ref, ATT_ROT_DIM // 2, ATT_HEAD_DIM).astype(BF16)

    @pl.when(is_plain)
    def _():
        o_ref[...] = acc.astype(BF16)

    @pl.when(j == CB_RG)
    def _():
        o_ref[...] = (acc * _sigmoid(acc)).astype(BF16)

    @pl.when(jnp.logical_or(j == CB_SGU, j == CB_SGV))
    def _():
        o_ref[...] = jax.nn.gelu(acc).astype(BF16)

    @pl.when(j >= CB_GATE)
    def _():
        o_ref[...] = _sigmoid(acc).astype(BF16)


def _rot_tables(S, n_rot, theta, period):
    half = n_rot // 2
    inv = jnp.power(jnp.float32(theta), -jnp.arange(half, dtype=F32) * (2.0 / n_rot))
    ang = jnp.arange(S, dtype=F32)[:, None] * inv[None, :]
    cos, sin = jnp.cos(ang), jnp.sin(ang)
    pad = period - n_rot
    cos_h = jnp.concatenate([cos, cos, jnp.ones((S, pad), F32)], axis=1)
    sin_h = jnp.concatenate([-sin, sin, jnp.zeros((S, pad), F32)], axis=1)
    rep = LANES // period
    return jnp.tile(cos_h, (1, rep)), jnp.tile(sin_h, (1, rep))


def _in_proj(x2d, gain, w_bf16, tabs, S, tm):
    T = x2d.shape[0]
    nrep = S // tm
    tab_spec = pl.BlockSpec((tm, LANES), lambda i, j: (i % nrep, 0))
    return pl.pallas_call(
        _in_proj_kernel,
        grid=(T // tm, N_COL_BLK),
        in_specs=[
            pl.BlockSpec((tm, D_MODEL), lambda i, j: (i, 0)),
            pl.BlockSpec((1, D_MODEL), lambda i, j: (0, 0)),
            pl.BlockSpec((D_MODEL, COL_BLK), lambda i, j: (0, j)),
            tab_spec, tab_spec, tab_spec, tab_spec,
        ],
        out_specs=pl.BlockSpec((tm, COL_BLK), lambda i, j: (i, j)),
        out_shape=jax.ShapeDtypeStruct((T, N_IN), BF16),
        scratch_shapes=[pltpu.VMEM((tm, D_MODEL), BF16)],
        compiler_params=_cparams(("arbitrary", "arbitrary")),
        name="in_proj",
    )(x2d, gain.reshape(1, D_MODEL), w_bf16, *tabs)


RET_TILE = 512
RET_CPT = RET_TILE // RET_CHUNK


def _retention_kernel(lg_ref, q_ref, k_ref, v_ref, g_ref, gn_ref, o_ref,
                      dmat_ref, tab_ref, sf_ref, sbrun_ref, sb_ref, *, n_tiles):
    b, ph, i = pl.program_id(0), pl.program_id(1), pl.program_id(2)
    C, H, DK, DV = RET_CHUNK, RET_HEADS, RET_QK_DIM, RET_V_DIM

    @pl.when(jnp.logical_and(b == 0, jnp.logical_and(ph == 0, i == 0)))
    def _():
        row = lax.broadcasted_iota(I32, (C, C), 0).astype(F32)
        col = lax.broadcasted_iota(I32, (C, C), 1).astype(F32)
        diff = row - col
        for h in range(H):
            fwd = jnp.exp(jnp.maximum(diff, 0.0) * lg_ref[0, h])
            bwd = jnp.exp(jnp.maximum(-diff, 0.0) * lg_ref[1, h])
            dmat_ref[h] = jnp.where(diff >= 0, fwd, bwd)
        pos = lax.broadcasted_iota(I32, (C, RET_QK_W), 0).astype(F32)
        head = lax.broadcasted_iota(I32, (C, RET_QK_W), 1) >> 6
        lgf = jnp.zeros((C, RET_QK_W), F32)
        lgb = jnp.zeros((C, RET_QK_W), F32)
        for h in range(H):
            lgf = jnp.where(head == h, lg_ref[0, h], lgf)
            lgb = jnp.where(head == h, lg_ref[1, h], lgb)
        tab_ref[0] = jnp.exp((C - 1.0 - pos) * lgf)
        tab_ref[1] = jnp.exp((pos + 1.0) * lgf)
        tab_ref[2] = jnp.exp(pos * lgb)
        tab_ref[3] = jnp.exp((C - pos) * lgb)
        tab_ref[4] = jnp.exp(C * lgf)
        tab_ref[5] = jnp.exp(C * lgb)

    def chunk_decay(idx):
        return tab_ref[idx][:DV, :].T

    def kv_state(kd, v):
        kdt = kd.T.astype(BF16)
        parts = [jnp.dot(kdt[h * DK:(h + 1) * DK, :], v[:, h * DV:(h + 1) * DV], preferred_element_type=F32)
                 for h in range(H)]
        return jnp.concatenate(parts, axis=0)

    @pl.when(ph == 0)
    def _():
        @pl.when(i == 0)
        def _():
            sbrun_ref[...] = jnp.zeros_like(sbrun_ref)

        tile = n_tiles - 1 - i
        dec = chunk_decay(5)
        for c in reversed(range(RET_CPT)):
            rows = pl.ds(c * C, C)
            sb_ref[tile * RET_CPT + c] = sbrun_ref[...]
            kd = k_ref[rows, :].astype(F32) * tab_ref[2]
            sbrun_ref[...] = dec * sbrun_ref[...] + kv_state(kd, v_ref[rows, :])

    @pl.when(ph == 1)
    def _():
        @pl.when(i == 0)
        def _():
            sf_ref[...] = jnp.zeros_like(sf_ref)

        dec = chunk_decay(4)
        head = lax.broadcasted_iota(I32, (C, RET_QK_W), 1) >> 6
        for c in range(RET_CPT):
            rows = pl.ds(c * C, C)
            q = q_ref[rows, :].astype(F32)
            k = k_ref[rows, :].astype(F32)
            v = v_ref[rows, :]
            kt = k.T.astype(BF16)
            qf = q * tab_ref[1]
            qb = q * tab_ref[3]
            sf = sf_ref[...].astype(BF16)
            sb = sb_ref[i * RET_CPT + c].astype(BF16)
            outs = []
            for h in range(H):
                sel = head == h
                qh = jnp.where(sel, q, 0.0).astype(BF16)
                s = jnp.dot(qh, kt, preferred_element_type=F32) * dmat_ref[h]
                o = jnp.dot(s.astype(BF16), v[:, h * DV:(h + 1) * DV], preferred_element_type=F32)
                o += jnp.dot(jnp.where(sel, qf, 0.0).astype(BF16), sf, preferred_element_type=F32)
                o += jnp.dot(jnp.where(sel, qb, 0.0).astype(BF16), sb, preferred_element_type=F32)
                mu = jnp.mean(o, axis=-1, keepdims=True)
                var = jnp.mean(jnp.square(o - mu), axis=-1, keepdims=True)
                outs.append((o - mu) * lax.rsqrt(var + NORM_EPS))
            y = jnp.concatenate(outs, axis=1) * gn_ref[...]
            o_ref[rows, :] = (y * g_ref[rows, :].astype(F32)).astype(BF16)
            sf_ref[...] = dec * sf_ref[...] + kv_state(k * tab_ref[0], v)


def _retention(proj, lg, gn_gain, B, S):
    nt = S // RET_TILE
    nc = S // RET_CHUNK
    qkw = RET_QK_W

    def kv_idx(blk):
        def f(b, ph, i, lg_ref):
            t = jnp.where(ph == 0, nt - 1 - i, i)
            return (b * nt + t, blk)
        return f

    def out_idx(blk):
        def f(b, ph, i, lg_ref):
            return (b * nt + i * ph, blk)
        return f

    grid_spec = pltpu.PrefetchScalarGridSpec(
        num_scalar_prefetch=1,
        grid=(B, 2, nt),
        in_specs=[
            pl.BlockSpec((RET_TILE, qkw), out_idx(0)),
            pl.BlockSpec((RET_TILE, qkw), kv_idx(1)),
            pl.BlockSpec((RET_TILE, RET_V_W), kv_idx(CB_RV)),
            pl.BlockSpec((RET_TILE, RET_V_W), out_idx(CB_RG)),
            pl.BlockSpec((1, RET_V_W), lambda b, ph, i, lg_ref: (0, 0)),
        ],
        out_specs=pl.BlockSpec((RET_TILE, RET_V_W), out_idx(0)),
        scratch_shapes=[
            pltpu.VMEM((RET_HEADS, RET_CHUNK, RET_CHUNK), F32),
            pltpu.VMEM((6, RET_CHUNK, qkw), F32),
            pltpu.VMEM((qkw, RET_V_DIM), F32),
            pltpu.VMEM((qkw, RET_V_DIM), F32),
            pltpu.VMEM((nc, qkw, RET_V_DIM), F32),
        ],
    )
    return pl.pallas_call(
        functools.partial(_retention_kernel, n_tiles=nt),
        grid_spec=grid_spec,
        out_shape=jax.ShapeDtypeStruct((B * S, RET_V_W), BF16),
        compiler_params=_cparams(("arbitrary", "arbitrary", "arbitrary")),
        name="retention",
    )(lg, proj, proj, proj, proj, gn_gain.reshape(1, RET_V_W))


ATT_QB = 128


def _dil_attn_kernel(q_ref, km_ref, kp_ref, kn_ref, vm_ref, vp_ref, vn_ref, o_ref, lse_ref, *, L, Lq):
    i = pl.program_id(2)
    R, QB, HD = ATT_R, ATT_QB, ATT_HEAD_DIM
    nq = Lq // QB
    NK = QB + 2 * R
    row = lax.broadcasted_iota(I32, (QB, NK), 0)
    col = lax.broadcasted_iota(I32, (QB, NK), 1)
    band = jnp.logical_and(col - row >= 0, col - row <= 2 * R)
    lane = lax.broadcasted_iota(I32, (QB, LANES), 1)
    scale = HD ** -0.5
    for qb in range(nq):
        base = i * Lq + qb * QB - R
        ok = jnp.logical_and(band, jnp.logical_and(col >= -base, col < L - base))

        def keys(main_ref, prev_ref, next_ref):
            lo = main_ref[qb * QB - R:qb * QB, :] if qb > 0 else prev_ref[...]
            hi = main_ref[(qb + 1) * QB:(qb + 1) * QB + R, :] if qb < nq - 1 else next_ref[...]
            return jnp.concatenate([lo, main_ref[qb * QB:(qb + 1) * QB, :], hi], axis=0)

        kc = keys(km_ref, kp_ref, kn_ref)
        vc = keys(vm_ref, vp_ref, vn_ref)
        lse_tile = jnp.zeros((QB, LANES), F32)
        for h in range(ATT_HEADS):
            cols = slice(h * HD, (h + 1) * HD)
            q = q_ref[qb * QB:(qb + 1) * QB, cols]
            s = lax.dot_general(q, kc[:, cols], (((1,), (1,)), ((), ())), preferred_element_type=F32) * scale
            s = jnp.where(ok, s, NEG_BIG)
            m = jnp.max(s, axis=-1, keepdims=True)
            p = jnp.exp(s - m)
            den = jnp.sum(p, axis=-1, keepdims=True)
            pn = (p * (1.0 / den)).astype(BF16)
            o = jnp.dot(pn, vc[:, cols], preferred_element_type=F32)
            o_ref[qb * QB:(qb + 1) * QB, cols] = o.astype(BF16)
            lse_tile = jnp.where((lane >> 5) == h, m + jnp.log(den), lse_tile)
        lse_ref[qb * QB:(qb + 1) * QB, :] = lse_tile


def _dil_attn(proj, g, B, S):
    window, d = ATT_PATTERNS[g]
    L = S // d
    Lq = min(L, 512)
    nl = L // Lq
    hb = Lq // ATT_R
    nhb = L // ATT_R
    cq = CB_ATT + 3 * g
    ck, cv = cq + 1, cq + 2
    pv = proj.reshape(B * L, d * N_IN)

    def main(cb):
        return pl.BlockSpec((Lq, COL_BLK), lambda b, r, i: (b * nl + i, r * N_COL_BLK + cb))

    def prev(cb):
        return pl.BlockSpec((ATT_R, COL_BLK), lambda b, r, i: (b * nhb + jnp.maximum(i * hb - 1, 0), r * N_COL_BLK + cb))

    def nxt(cb):
        return pl.BlockSpec((ATT_R, COL_BLK),
                            lambda b, r, i: (b * nhb + jnp.minimum((i + 1) * hb, nhb - 1), r * N_COL_BLK + cb))

    o, lse = pl.pallas_call(
        functools.partial(_dil_attn_kernel, L=L, Lq=Lq),
        grid=(B, d, nl),
        in_specs=[main(cq), main(ck), prev(ck), nxt(ck), main(cv), prev(cv), nxt(cv)],
        out_specs=[
            pl.BlockSpec((Lq, ATT_W), lambda b, r, i: (b * nl + i, r)),
            pl.BlockSpec((Lq, LANES), lambda b, r, i: (b * nl + i, r)),
        ],
        out_shape=[
            jax.ShapeDtypeStruct((B * L, d * ATT_W), BF16),
            jax.ShapeDtypeStruct((B * L, d * LANES), F32),
        ],
        compiler_params=_cparams(("arbitrary", "arbitrary", "arbitrary")),
        name=f"dil_attn_{g}",
    )(pv, pv, pv, pv, pv, pv, pv)
    return o.reshape(B * S, ATT_W), lse.reshape(B * S, LANES)


MERGE_TILE = 512


def _merge_kernel(x_ref, r_ref, o0_ref, o1_ref, o2_ref, l0_ref, l1_ref, l2_ref, su_ref, sv_ref,
                  g0_ref, g1_ref, g2_ref, wr_ref, wa_ref, ws_ref, wo_ref, sgw_ref, sgb_ref, lng_ref,
                  nf_ref, wrt_ref, brt_ref, tri_ref,
                  x2_ref, hp_ref, route_ref, cnt_ref, carry_ref):
    i = pl.program_id(0)
    tm = x_ref.shape[0]
    HD = ATT_HEAD_DIM

    @pl.when(i == 0)
    def _():
        carry_ref[...] = jnp.zeros_like(carry_ref)

    l0, l1, l2 = l0_ref[...], l1_ref[...], l2_ref[...]
    m = jnp.maximum(jnp.maximum(l0, l1), l2)
    e0, e1, e2 = jnp.exp(l0 - m), jnp.exp(l1 - m), jnp.exp(l2 - m)
    inv = 1.0 / (e0 + e1 + e2)
    heads = []
    per_head = LANES // ATT_HEADS
    for h in range(ATT_HEADS):
        cols = slice(h * HD, (h + 1) * HD)
        c = slice(h * per_head, h * per_head + 1)
        a = (e0 * inv)[:, c] * o0_ref[:, cols].astype(F32)
        a += (e1 * inv)[:, c] * o1_ref[:, cols].astype(F32)
        a += (e2 * inv)[:, c] * o2_ref[:, cols].astype(F32)
        heads.append(a)
    att = jnp.concatenate(heads, axis=1).astype(BF16)
    y_att = jnp.dot(att, wa_ref[...], preferred_element_type=F32)
    y_ret = jnp.dot(r_ref[...], wr_ref[...], preferred_element_type=F32)

    u = su_ref[...].astype(F32)
    v = sv_ref[...].astype(F32)
    mu = jnp.mean(v, axis=-1, keepdims=True)
    var = jnp.mean(jnp.square(v - mu), axis=-1, keepdims=True)
    vn = ((v - mu) * lax.rsqrt(var + NORM_EPS) * lng_ref[...]).astype(BF16)
    gw = SG_WIDTH // SG_GROUPS
    chunks = []
    for c in range(tm // SG_CHUNK):
        rows = slice(c * SG_CHUNK, (c + 1) * SG_CHUNK)
        parts = [jnp.dot(sgw_ref[g], vn[rows, g * gw:(g + 1) * gw], preferred_element_type=F32)
                 for g in range(SG_GROUPS)]
        chunks.append(jnp.concatenate(parts, axis=1) + sgb_ref[...])
    vs = jnp.concatenate(chunks, axis=0)
    y_sg = jnp.dot((u * vs).astype(BF16), ws_ref[...], preferred_element_type=F32)

    merged = (g0_ref[...].astype(F32) * y_ret + g1_ref[...].astype(F32) * y_att
              + g2_ref[...].astype(F32) * y_sg)
    x2 = x_ref[...] + jnp.dot(merged.astype(BF16), wo_ref[...], preferred_element_type=F32)
    x2_ref[...] = x2

    ms = jnp.mean(x2 * x2, axis=-1, keepdims=True)
    h2 = x2 * lax.rsqrt(ms + NORM_EPS) * nf_ref[...]
    bits = lax.bitcast_convert_type(h2.astype(BF16).astype(F32), U32)
    half = D_MODEL // 2
    hp_ref[...] = (bits[:, :half] >> 16) | (bits[:, half:] & jnp.uint32(0xFFFF0000))

    logits = jnp.dot(h2, wrt_ref[...], preferred_element_type=F32, precision=lax.Precision.HIGHEST) + brt_ref[...]
    lane = lax.broadcasted_iota(I32, logits.shape, 1)
    big = jnp.int32(LANES)
    is_g = lane < N_GROUPS
    gl = jnp.where(is_g, logits, -jnp.inf)
    gmax = jnp.max(gl, axis=-1, keepdims=True)
    g_idx = jnp.min(jnp.where(jnp.logical_and(is_g, gl == gmax), lane, big), axis=-1, keepdims=True)
    g_w = 1.0 / jnp.sum(jnp.where(is_g, jnp.exp(gl - gmax), 0.0), axis=-1, keepdims=True)
    lo = ROUTE_LANE0 + g_idx * EXP_PER_GROUP
    in_grp = jnp.logical_and(lane >= lo, lane < lo + EXP_PER_GROUP)
    el = jnp.where(in_grp, logits, -jnp.inf)
    v1 = jnp.max(el, axis=-1, keepdims=True)
    i1 = jnp.min(jnp.where(el == v1, lane, big), axis=-1, keepdims=True)
    el2 = jnp.where(lane == i1, -jnp.inf, el)
    v2 = jnp.max(el2, axis=-1, keepdims=True)
    i2 = jnp.min(jnp.where(el2 == v2, lane, big), axis=-1, keepdims=True)
    t = jnp.exp(v2 - v1)
    w1 = g_w / (1.0 + t)
    w2 = g_w * t / (1.0 + t)

    oh1 = lane == i1
    oh2 = lane == i2
    both = jnp.where(jnp.logical_or(oh1, oh2), 1.0, 0.0)
    before = jnp.dot(tri_ref[...], both.astype(BF16), preferred_element_type=F32) + carry_ref[0:1, :]
    rank1 = jnp.sum(jnp.where(oh1, before, 0.0), axis=-1, keepdims=True)
    rank2 = jnp.sum(jnp.where(oh2, before, 0.0), axis=-1, keepdims=True)
    carry = carry_ref[0:1, :] + jnp.sum(both, axis=0, keepdims=True)
    carry_ref[...] = jnp.broadcast_to(carry, carry_ref.shape)
    cnt_ref[...] = jnp.broadcast_to(carry, cnt_ref.shape)

    out = jnp.where(lane == 0, (i1 - ROUTE_LANE0).astype(F32), 0.0)
    out = jnp.where(lane == 1, (i2 - ROUTE_LANE0).astype(F32), out)
    out = jnp.where(lane == 2, w1, out)
    out = jnp.where(lane == 3, w2, out)
    out = jnp.where(lane == 4, rank1, out)
    out = jnp.where(lane == 5, rank2, out)
    route_ref[...] = out


def _merge(x2d, r, os_, lses, proj, lw):
    T = x2d.shape[0]
    tm = MERGE_TILE
    row = lambda w: pl.BlockSpec((tm, w), lambda i: (i, 0))
    pcol = lambda w, cb: pl.BlockSpec((tm, w), lambda i: (i, cb))
    full = lambda a: pl.BlockSpec(a.shape, lambda i: (0,) * a.ndim)
    gate_cb = CB_GATE * COL_BLK // D_MODEL
    weights = [lw["w_ret_out"], lw["w_att_out"], lw["w_sg_out"], lw["w_o"], lw["sg_w"], lw["sg_b_full"],
               lw["sg_ln_gain"], lw["norm_ffn"], lw["w_router"], lw["b_router"], lw["tri"]]
    return pl.pallas_call(
        _merge_kernel,
        grid=(T // tm,),
        in_specs=[row(D_MODEL), row(RET_V_W), row(ATT_W), row(ATT_W), row(ATT_W), row(LANES), row(LANES), row(LANES),
                  pcol(SG_WIDTH, CB_SGU), pcol(SG_WIDTH, CB_SGV),
                  pcol(D_MODEL, gate_cb), pcol(D_MODEL, gate_cb + 1), pcol(D_MODEL, gate_cb + 2)]
                 + [full(a) for a in weights],
        out_specs=[row(D_MODEL), row(D_MODEL // 2), row(LANES), pl.BlockSpec((8, LANES), lambda i: (0, 0))],
        out_shape=[jax.ShapeDtypeStruct((T, D_MODEL), F32), jax.ShapeDtypeStruct((T, D_MODEL // 2), U32),
                   jax.ShapeDtypeStruct((T, LANES), F32), jax.ShapeDtypeStruct((8, LANES), F32)],
        scratch_shapes=[pltpu.VMEM((8, LANES), F32)],
        compiler_params=_cparams(("arbitrary",)),
        name="merge",
    )(x2d, r, *os_, *lses, proj, proj, proj, proj, proj, *weights)


DISPATCH_TILE = 1024


def _dispatch_kernel(dest_ref, h_ref, xb_in_ref, xb_ref, sem):
    del xb_in_ref
    i = pl.program_id(0)
    tm = DISPATCH_TILE

    def issue(t, carry):
        src = h_ref.at[pl.ds(i * tm + t, 1)]
        for k in range(TOP_K):
            pltpu.make_async_copy(src, xb_ref.at[pl.ds(dest_ref[0, 0, TOP_K * t + k], 1)], sem).start()
        return carry

    lax.fori_loop(0, tm, issue, 0)
    pltpu.make_async_copy(xb_ref.at[pl.ds(0, TOP_K * tm)], xb_ref.at[pl.ds(0, TOP_K * tm)], sem).wait()


def _dispatch(hp, dest, P):
    T = hp.shape[0]
    tm = DISPATCH_TILE
    half = D_MODEL // 2
    return pl.pallas_call(
        _dispatch_kernel,
        grid=(T // tm,),
        in_specs=[
            pl.BlockSpec((1, 1, TOP_K * tm), lambda i: (i, 0, 0), memory_space=pltpu.SMEM),
            pl.BlockSpec(memory_space=pl.ANY),
            pl.BlockSpec(memory_space=pl.ANY),
        ],
        out_specs=pl.BlockSpec(memory_space=pl.ANY),
        out_shape=jax.ShapeDtypeStruct((P, half), U32),
        scratch_shapes=[pltpu.SemaphoreType.DMA(())],
        input_output_aliases={2: 0},
        compiler_params=_cparams(("arbitrary",)),
        name="dispatch",
    )(dest.reshape(T // tm, 1, TOP_K * tm), hp, jnp.zeros((P, half), U32))


def _experts_kernel(blk_e_ref, nused_ref, x_ref, wg_ref, wu_ref, wd_ref, y_ref):
    i = pl.program_id(0)

    @pl.when(i < nused_ref[0])
    def _():
        pk = x_ref[...]
        lo = lax.bitcast_convert_type(pk << 16, F32)
        hi = lax.bitcast_convert_type(pk & jnp.uint32(0xFFFF0000), F32)
        x = jnp.concatenate([lo, hi], axis=1).astype(BF16)
        gate = jnp.dot(x, wg_ref[0], preferred_element_type=F32)
        up = jnp.dot(x, wu_ref[0], preferred_element_type=F32)
        hid = (gate * _sigmoid(gate) * up).astype(BF16)
        y_ref[...] = jnp.dot(hid, wd_ref[0], preferred_element_type=F32)

    @pl.when(i >= nused_ref[0])
    def _():
        y_ref[...] = jnp.zeros_like(y_ref)


def _experts(xb, blk_e, nused, wg, wu, wd):
    P = xb.shape[0]
    bm = MOE_BLOCK
    grid_spec = pltpu.PrefetchScalarGridSpec(
        num_scalar_prefetch=2,
        grid=(P // bm,),
        in_specs=[
            pl.BlockSpec((bm, D_MODEL // 2), lambda i, e, n: (i, 0)),
            pl.BlockSpec((1, D_MODEL, EXP_HIDDEN), lambda i, e, n: (e[i], 0, 0)),
            pl.BlockSpec((1, D_MODEL, EXP_HIDDEN), lambda i, e, n: (e[i], 0, 0)),
            pl.BlockSpec((1, EXP_HIDDEN, D_MODEL), lambda i, e, n: (e[i], 0, 0)),
        ],
        out_specs=pl.BlockSpec((bm, D_MODEL), lambda i, e, n: (i, 0)),
    )
    return pl.pallas_call(
        _experts_kernel,
        grid_spec=grid_spec,
        out_shape=jax.ShapeDtypeStruct((P, D_MODEL), F32),
        compiler_params=_cparams(("arbitrary",)),
        name="experts",
    )(blk_e, nused, xb, wg, wu, wd)


COMBINE_TILE = 256


def _combine_kernel(dest_ref, x_ref, route_ref, gain_ref, y_ref, o_ref, buf_ref, sem, *, final_norm):
    tm = COMBINE_TILE

    def issue(t, carry):
        for k in range(TOP_K):
            pltpu.make_async_copy(y_ref.at[pl.ds(dest_ref[0, 0, TOP_K * t + k], 1)],
                                  buf_ref.at[k, pl.ds(t, 1)], sem).start()
        return carry

    lax.fori_loop(0, tm, issue, 0)
    pltpu.make_async_copy(buf_ref, buf_ref, sem).wait()
    route = route_ref[...]
    x = x_ref[...] + route[:, 2:3] * buf_ref[0] + route[:, 3:4] * buf_ref[1]
    if final_norm:
        ms = jnp.mean(x * x, axis=-1, keepdims=True)
        x = x * lax.rsqrt(ms + NORM_EPS) * gain_ref[...]
    o_ref[...] = x


def _combine(x2, route, yb, dest, gain, final_norm):
    T = x2.shape[0]
    tm = COMBINE_TILE
    return pl.pallas_call(
        functools.partial(_combine_kernel, final_norm=final_norm),
        grid=(T // tm,),
        in_specs=[
            pl.BlockSpec((1, 1, TOP_K * tm), lambda i: (i, 0, 0), memory_space=pltpu.SMEM),
            pl.BlockSpec((tm, D_MODEL), lambda i: (i, 0)),
            pl.BlockSpec((tm, LANES), lambda i: (i, 0)),
            pl.BlockSpec((1, D_MODEL), lambda i: (0, 0)),
            pl.BlockSpec(memory_space=pl.ANY),
        ],
        out_specs=pl.BlockSpec((tm, D_MODEL), lambda i: (i, 0)),
        out_shape=jax.ShapeDtypeStruct((T, D_MODEL), F32),
        scratch_shapes=[pltpu.VMEM((TOP_K, tm, D_MODEL), F32), pltpu.SemaphoreType.DMA(())],
        compiler_params=_cparams(("arbitrary",)),
        name="combine_final" if final_norm else "combine",
    )(dest.reshape(T // tm, 1, TOP_K * tm), x2, route, gain.reshape(1, D_MODEL), yb)


def _layer_weights(l, p):
    w_router = jnp.zeros((D_MODEL, LANES), F32)
    w_router = w_router.at[:, :N_GROUPS].set(p["w_router_group"][l])
    w_router = w_router.at[:, ROUTE_LANE0:ROUTE_LANE0 + N_EXPERTS].set(p["w_router_expert"][l])
    b_router = jnp.zeros((1, LANES), F32)
    b_router = b_router.at[0, :N_GROUPS].set(p["b_router_group"][l])
    b_router = b_router.at[0, ROUTE_LANE0:ROUTE_LANE0 + N_EXPERTS].set(p["b_router_expert"][l])
    tm = MERGE_TILE
    tri = (jnp.arange(tm)[:, None] > jnp.arange(tm)[None, :]).astype(BF16)
    gw = SG_WIDTH // SG_GROUPS
    return {
        "norm_mix": p["norm_mix"][l],
        "w_in": p["w_in"][l].astype(BF16),
        "lg": jnp.stack([-jnp.exp(p["ret_decay_fwd"][l].astype(F32)), -jnp.exp(p["ret_decay_bwd"][l].astype(F32))]),
        "ret_gn_gain": p["ret_gn_gain"][l],
        "w_ret_out": p["w_ret_out"][l].astype(BF16),
        "w_att_out": p["w_att_out"][l].astype(BF16),
        "w_sg_out": p["w_sg_out"][l].astype(BF16),
        "w_o": p["w_o"][l].astype(BF16),
        "sg_w": p["sg_w"][l].astype(BF16),
        "sg_b_full": jnp.repeat(p["sg_b"][l].T, gw, axis=1),
        "sg_ln_gain": p["sg_ln_gain"][l].reshape(1, SG_WIDTH),
        "norm_ffn": p["norm_ffn"][l].reshape(1, D_MODEL),
        "w_router": w_router,
        "b_router": b_router,
        "tri": tri,
        "w_exp_gate": p["w_exp_gate"][l].astype(BF16),
        "w_exp_up": p["w_exp_up"][l].astype(BF16),
        "w_exp_down": p["w_exp_down"][l].astype(BF16),
    }


def _moe_plan(route, cnt, T):
    bm = MOE_BLOCK
    eid = route[:, 0:TOP_K].astype(I32)
    rank = route[:, 4:4 + TOP_K].astype(I32)
    counts = cnt[0, ROUTE_LANE0:ROUTE_LANE0 + N_EXPERTS].astype(I32)
    padded = (counts + bm - 1) // bm * bm
    pad_end = jnp.cumsum(padded)
    pad_start = pad_end - padded
    dest = pad_start[eid] + rank
    P = T * TOP_K + N_EXPERTS * bm
    nblk = P // bm
    blk_e = jnp.minimum(jnp.searchsorted(pad_end, jnp.arange(nblk, dtype=I32) * bm, side="right"),
                        N_EXPERTS - 1).astype(I32)
    nused = (pad_end[-1:] // bm).astype(I32)
    return dest.reshape(-1), blk_e, nused, P


def _run_trunk(x, lws, tabs, norm_final):
    B, S, _ = x.shape
    T = B * S
    x2d = x.reshape(T, D_MODEL)
    tm_in = min(1024, S)
    for l, lw in enumerate(lws):
        proj = _in_proj(x2d, lw["norm_mix"], lw["w_in"], tabs, S, tm_in)
        r = _retention(proj, lw["lg"], lw["ret_gn_gain"], B, S)
        os_, lses = zip(*[_dil_attn(proj, g, B, S) for g in range(len(ATT_PATTERNS))])
        x2, hp, route, cnt = _merge(x2d, r, os_, lses, proj, lw)
        dest, blk_e, nused, P = _moe_plan(route, cnt, T)
        xb = _dispatch(hp, dest, P)
        yb = _experts(xb, blk_e, nused, lw["w_exp_gate"], lw["w_exp_up"], lw["w_exp_down"])
        x2d = _combine(x2, route, yb, dest, norm_final, final_norm=(l == len(lws) - 1))
    return x2d.reshape(B, S, D_MODEL)


def kernel(x_prompt, x_sample, norm_mix, w_in, ret_decay_fwd, ret_decay_bwd, ret_gn_gain, sg_ln_gain, sg_w, sg_b,
           w_ret_out, w_att_out, w_sg_out, w_o, norm_ffn, w_router_group, b_router_group, w_router_expert,
           b_router_expert, w_exp_gate, w_exp_up, w_exp_down, norm_final):
    p = dict(norm_mix=norm_mix, w_in=w_in, ret_decay_fwd=ret_decay_fwd, ret_decay_bwd=ret_decay_bwd,
             ret_gn_gain=ret_gn_gain, sg_ln_gain=sg_ln_gain, sg_w=sg_w, sg_b=sg_b, w_ret_out=w_ret_out,
             w_att_out=w_att_out, w_sg_out=w_sg_out, w_o=w_o, norm_ffn=norm_ffn, w_router_group=w_router_group,
             b_router_group=b_router_group, w_router_expert=w_router_expert, b_router_expert=b_router_expert,
             w_exp_gate=w_exp_gate, w_exp_up=w_exp_up, w_exp_down=w_exp_down)
    lws = [_layer_weights(l, p) for l in range(DEPTH)]
    s_max = max(x_prompt.shape[1], x_sample.shape[1])
    tabs = _rot_tables(s_max, RET_QK_DIM, RET_THETA, RET_QK_DIM) + _rot_tables(s_max, ATT_ROT_DIM, ROPE_THETA, ATT_HEAD_DIM)
    y_prompt = _run_trunk(x_prompt, lws, tabs, norm_final)
    y_sample = _run_trunk(x_sample, lws, tabs, norm_final)
    return (y_prompt, y_sample)
```

```python
import functools

import jax
import jax.numpy as jnp
import numpy as np
from jax import lax
from jax.experimental import pallas as pl
from jax.experimental.pallas import tpu as pltpu

F32 = jnp.float32
BF16 = jnp.bfloat16
U32 = jnp.uint32
I32 = jnp.int32

D_MODEL = 1024
DEPTH = 2
NORM_EPS = 1e-6
RET_HEADS = 4
RET_QK_DIM = 64
RET_V_DIM = 128
RET_CHUNK = 128
RET_THETA = 10000.0
ATT_PATTERNS = ((128, 1), (512, 4), (2048, 16))
ATT_HEADS = 4
ATT_HEAD_DIM = 128
ATT_ROT_DIM = ATT_HEAD_DIM // 4
ROPE_THETA = 500000.0
SG_GROUPS = 4
SG_CHUNK = 128
SG_WIDTH = 512
N_BRANCH = 3
N_GROUPS = 4
EXP_PER_GROUP = 8
N_EXPERTS = N_GROUPS * EXP_PER_GROUP
TOP_K = 2
EXP_HIDDEN = 512
RET_QK_W = RET_HEADS * RET_QK_DIM
RET_V_W = RET_HEADS * RET_V_DIM
ATT_W = ATT_HEADS * ATT_HEAD_DIM
ATT_IN_W = len(ATT_PATTERNS) * 3 * ATT_W
SG_IN_W = 2 * SG_WIDTH
GATE_IN_W = N_BRANCH * D_MODEL
N_IN = 2 * RET_QK_W + 2 * RET_V_W + ATT_IN_W + SG_IN_W + GATE_IN_W
NEG_BIG = -1e30

LANES = 128
COL_BLK = 512
N_COL_BLK = N_IN // COL_BLK
ATT_R = 64
assert all(w // (2 * d) == ATT_R for w, d in ATT_PATTERNS)
CB_GATE, CB_RV, CB_RG, CB_SGU, CB_SGV, CB_RQK, CB_ATT = 0, 6, 7, 8, 9, 10, 11
N_PROJ_BLK = CB_ATT
N_ATT_BLK = N_COL_BLK - CB_ATT
N_ROT_BLK = 1 + 2 * len(ATT_PATTERNS)
ATTP_BLK = COL_BLK // 2
ROUTE_LANE0 = N_GROUPS
MOE_BLOCK = 256
VMEM_LIMIT = 56 * 1024 * 1024
HI_HALF = 0xFFFF0000


def _cparams(sem, vmem=VMEM_LIMIT):
    return pltpu.CompilerParams(dimension_semantics=sem, vmem_limit_bytes=vmem)


def _sigmoid(x):
    return 0.5 * jnp.tanh(0.5 * x) + 0.5


def _pack_halves(y):
    n = y.shape[1] // 2
    bits = lax.bitcast_convert_type(y.astype(BF16).astype(F32), U32)
    return (bits[:, :n] >> 16) | (bits[:, n:] & jnp.uint32(HI_HALF))


def _unpack_halves(p):
    return (lax.bitcast_convert_type(p << 16, F32), lax.bitcast_convert_type(p & jnp.uint32(HI_HALF), F32))


def _in_proj_kernel(x_ref, g_ref, w_ref, wrot_ref, rc_ref, rs_ref, ac_ref, as_ref, o_ref, oa_ref, h_ref):
    j = pl.program_id(1)

    @pl.when(j == 0)
    def _():
        x = x_ref[...]
        ms = jnp.mean(x * x, axis=-1, keepdims=True)
        h_ref[...] = (x * lax.rsqrt(ms + NORM_EPS) * g_ref[...]).astype(BF16)

    def dot(w):
        return jnp.dot(h_ref[...], w[...], preferred_element_type=F32)

    def rotated(cos_ref, sin_ref):
        reps = COL_BLK // LANES
        cos = jnp.concatenate([cos_ref[...]] * reps, axis=1)
        sin = jnp.concatenate([sin_ref[...]] * reps, axis=1)
        return dot(w_ref) * cos + dot(wrot_ref) * sin

    att_kind = (j - CB_ATT) % 3

    @pl.when(j < CB_RV)
    def _():
        o_ref[...] = _sigmoid(dot(w_ref)).astype(BF16)

    @pl.when(j == CB_RV)
    def _():
        o_ref[...] = dot(w_ref).astype(BF16)

    @pl.when(j == CB_RG)
    def _():
        acc = dot(w_ref)
        o_ref[...] = (acc * _sigmoid(acc)).astype(BF16)

    @pl.when(jnp.logical_or(j == CB_SGU, j == CB_SGV))
    def _():
        o_ref[...] = jax.nn.gelu(dot(w_ref)).astype(BF16)

    @pl.when(j == CB_RQK)
    def _():
        y = rotated(rc_ref, rs_ref)
        lane = lax.broadcasted_iota(I32, y.shape, 1)
        y = jnp.where(lane >= RET_QK_W, y * (RET_QK_DIM ** -0.5), y)
        o_ref[...] = y.astype(BF16)

    @pl.when(jnp.logical_and(j >= CB_ATT, att_kind != 2))
    def _():
        oa_ref[...] = _pack_halves(rotated(ac_ref, as_ref))

    @pl.when(jnp.logical_and(j >= CB_ATT, att_kind == 2))
    def _():
        oa_ref[...] = _pack_halves(dot(w_ref))


def _rot_tables(S, n_rot, theta, period):
    half = n_rot // 2
    inv = jnp.power(jnp.float32(theta), -jnp.arange(half, dtype=F32) * (2.0 / n_rot))
    ang = jnp.arange(S, dtype=F32)[:, None] * inv[None, :]
    cos, sin = jnp.cos(ang), jnp.sin(ang)
    pad = period - n_rot
    cos_h = jnp.concatenate([cos, cos, jnp.ones((S, pad), F32)], axis=1)
    sin_h = jnp.concatenate([sin, sin, jnp.zeros((S, pad), F32)], axis=1)
    rep = LANES // period
    return jnp.tile(cos_h, (1, rep)), jnp.tile(sin_h, (1, rep))


def _rot_partner_cols(w, n_rot, period):
    half = n_rot // 2
    k, n = w.shape
    wh = w.reshape(k, n // period, period)
    out = jnp.concatenate([-wh[..., half:n_rot], wh[..., :half], jnp.zeros_like(wh[..., n_rot:])], axis=-1)
    return out.reshape(k, n)


def _in_proj_weights(w_in):
    o = np.cumsum([0, RET_QK_W, RET_QK_W, RET_V_W, RET_V_W, ATT_IN_W, SG_IN_W, GATE_IN_W])
    rqk, rv, rg, att, sg, gate = (w_in[:, o[0]:o[2]], w_in[:, o[2]:o[3]], w_in[:, o[3]:o[4]], w_in[:, o[4]:o[5]],
                                  w_in[:, o[5]:o[6]], w_in[:, o[6]:o[7]])
    w = jnp.concatenate([gate, rv, rg, sg, rqk, att], axis=1)
    rots = [_rot_partner_cols(rqk, RET_QK_DIM, RET_QK_DIM)]
    for g in range(len(ATT_PATTERNS)):
        for kind in range(2):
            c0 = (3 * g + kind) * ATT_W
            rots.append(_rot_partner_cols(att[:, c0:c0 + ATT_W], ATT_ROT_DIM, ATT_HEAD_DIM))
    return w.astype(BF16), jnp.concatenate(rots, axis=1).astype(BF16)


def _in_proj(x2d, gain, w_bf16, wrot_bf16, tabs, S, tm):
    T = x2d.shape[0]
    nrep = S // tm
    tab_spec = pl.BlockSpec((tm, LANES), lambda i, j: (i % nrep, 0))

    def rot_idx(i, j):
        a = jnp.maximum(j - CB_ATT, 0)
        return (0, jnp.where(j >= CB_ATT, 1 + 2 * (a // 3) + jnp.minimum(a % 3, 1), 0))

    return pl.pallas_call(
        _in_proj_kernel,
        grid=(T // tm, N_COL_BLK),
        in_specs=[
            pl.BlockSpec((tm, D_MODEL), lambda i, j: (i, 0)),
            pl.BlockSpec((1, D_MODEL), lambda i, j: (0, 0)),
            pl.BlockSpec((D_MODEL, COL_BLK), lambda i, j: (0, j)),
            pl.BlockSpec((D_MODEL, COL_BLK), rot_idx),
            tab_spec, tab_spec, tab_spec, tab_spec,
        ],
        out_specs=[
            pl.BlockSpec((tm, COL_BLK), lambda i, j: (i, jnp.minimum(j, N_PROJ_BLK - 1))),
            pl.BlockSpec((tm, ATTP_BLK), lambda i, j: (i, jnp.maximum(j - CB_ATT, 0))),
        ],
        out_shape=[
            jax.ShapeDtypeStruct((T, N_PROJ_BLK * COL_BLK), BF16),
            jax.ShapeDtypeStruct((T, N_ATT_BLK * ATTP_BLK), U32),
        ],
        scratch_shapes=[pltpu.VMEM((tm, D_MODEL), BF16)],
        compiler_params=_cparams(("arbitrary", "arbitrary")),
        name="in_proj",
    )(x2d, gain.reshape(1, D_MODEL), w_bf16, wrot_bf16, *tabs)


RET_TILE = 512
RET_CPT = RET_TILE // RET_CHUNK


def _retention_kernel(lg_ref, q_ref, k_ref, v_ref, g_ref, gn_ref, o_ref,
                      dmat_ref, tab_ref, sf_ref, sbrun_ref, sb_ref, *, n_tiles):
    b, ph, i = pl.program_id(0), pl.program_id(1), pl.program_id(2)
    C, H, DK, DV = RET_CHUNK, RET_HEADS, RET_QK_DIM, RET_V_DIM

    @pl.when(jnp.logical_and(b == 0, jnp.logical_and(ph == 0, i == 0)))
    def _():
        row = lax.broadcasted_iota(I32, (C, C), 0).astype(F32)
        col = lax.broadcasted_iota(I32, (C, C), 1).astype(F32)
        diff = row - col
        for h in range(H):
            fwd = jnp.exp(jnp.maximum(diff, 0.0) * lg_ref[0, h])
            bwd = jnp.exp(jnp.maximum(-diff, 0.0) * lg_ref[1, h])
            dmat_ref[h] = jnp.where(diff >= 0, fwd, bwd)
        pos = lax.broadcasted_iota(I32, (C, RET_QK_W), 0).astype(F32)
        head = lax.broadcasted_iota(I32, (C, RET_QK_W), 1) >> 6
        lgf = jnp.zeros((C, RET_QK_W), F32)
        lgb = jnp.zeros((C, RET_QK_W), F32)
        for h in range(H):
            lgf = jnp.where(head == h, lg_ref[0, h], lgf)
            lgb = jnp.where(head == h, lg_ref[1, h], lgb)
        tab_ref[0] = jnp.exp((C - 1.0 - pos) * lgf)
        tab_ref[1] = jnp.exp((pos + 1.0) * lgf)
        tab_ref[2] = jnp.exp(pos * lgb)
        tab_ref[3] = jnp.exp((C - pos) * lgb)
        tab_ref[4] = jnp.exp(C * lgf)
        tab_ref[5] = jnp.exp(C * lgb)

    def chunk_decay(idx):
        return tab_ref[idx][:DV, :].T

    def kv_state(kd, v):
        kdt = kd.T.astype(BF16)
        parts = [jnp.dot(kdt[h * DK:(h + 1) * DK, :], v[:, h * DV:(h + 1) * DV], preferred_element_type=F32)
                 for h in range(H)]
        return jnp.concatenate(parts, axis=0)

    @pl.when(ph == 0)
    def _():
        @pl.when(i == 0)
        def _():
            sbrun_ref[...] = jnp.zeros_like(sbrun_ref)

        tile = n_tiles - 1 - i
        dec = chunk_decay(5)
        for c in reversed(range(RET_CPT)):
            rows = pl.ds(c * C, C)
            sb_ref[tile * RET_CPT + c] = sbrun_ref[...]
            kd = k_ref[rows, :].astype(F32) * tab_ref[2]
            sbrun_ref[...] = dec * sbrun_ref[...] + kv_state(kd, v_ref[rows, :])

    @pl.when(ph == 1)
    def _():
        @pl.when(i == 0)
        def _():
            sf_ref[...] = jnp.zeros_like(sf_ref)

        dec = chunk_decay(4)
        head = lax.broadcasted_iota(I32, (C, RET_QK_W), 1) >> 6
        for c in range(RET_CPT):
            rows = pl.ds(c * C, C)
            q = q_ref[rows, :].astype(F32)
            k = k_ref[rows, :].astype(F32)
            v = v_ref[rows, :]
            kt = k.T.astype(BF16)
            qf = q * tab_ref[1]
            qb = q * tab_ref[3]
            sf = sf_ref[...].astype(BF16)
            sb = sb_ref[i * RET_CPT + c].astype(BF16)
            outs = []
            for h in range(H):
                sel = head == h
                qh = jnp.where(sel, q, 0.0).astype(BF16)
                s = jnp.dot(qh, kt, preferred_element_type=F32) * dmat_ref[h]
                o = jnp.dot(s.astype(BF16), v[:, h * DV:(h + 1) * DV], preferred_element_type=F32)
                o += jnp.dot(jnp.where(sel, qf, 0.0).astype(BF16), sf, preferred_element_type=F32)
                o += jnp.dot(jnp.where(sel, qb, 0.0).astype(BF16), sb, preferred_element_type=F32)
                mu = jnp.mean(o, axis=-1, keepdims=True)
                var = jnp.mean(jnp.square(o - mu), axis=-1, keepdims=True)
                outs.append((o - mu) * lax.rsqrt(var + NORM_EPS))
            y = jnp.concatenate(outs, axis=1) * gn_ref[...]
            o_ref[rows, :] = (y * g_ref[rows, :].astype(F32)).astype(BF16)
            sf_ref[...] = dec * sf_ref[...] + kv_state(k * tab_ref[0], v)


def _retention(proj, lg, gn_gain, B, S):
    nt = S // RET_TILE
    nc = S // RET_CHUNK
    qkw = RET_QK_W

    def kv_idx(blk):
        def f(b, ph, i, lg_ref):
            t = jnp.where(ph == 0, nt - 1 - i, i)
            return (b * nt + t, blk)
        return f

    def out_idx(blk):
        def f(b, ph, i, lg_ref):
            return (b * nt + i * ph, blk)
        return f

    grid_spec = pltpu.PrefetchScalarGridSpec(
        num_scalar_prefetch=1,
        grid=(B, 2, nt),
        in_specs=[
            pl.BlockSpec((RET_TILE, qkw), out_idx(2 * CB_RQK)),
            pl.BlockSpec((RET_TILE, qkw), kv_idx(2 * CB_RQK + 1)),
            pl.BlockSpec((RET_TILE, RET_V_W), kv_idx(CB_RV)),
            pl.BlockSpec((RET_TILE, RET_V_W), out_idx(CB_RG)),
            pl.BlockSpec((1, RET_V_W), lambda b, ph, i, lg_ref: (0, 0)),
        ],
        out_specs=pl.BlockSpec((RET_TILE, RET_V_W), out_idx(0)),
        scratch_shapes=[
            pltpu.VMEM((RET_HEADS, RET_CHUNK, RET_CHUNK), F32),
            pltpu.VMEM((6, RET_CHUNK, qkw), F32),
            pltpu.VMEM((qkw, RET_V_DIM), F32),
            pltpu.VMEM((qkw, RET_V_DIM), F32),
            pltpu.VMEM((nc, qkw, RET_V_DIM), F32),
        ],
    )
    return pl.pallas_call(
        functools.partial(_retention_kernel, n_tiles=nt),
        grid_spec=grid_spec,
        out_shape=jax.ShapeDtypeStruct((B * S, RET_V_W), BF16),
        compiler_params=_cparams(("arbitrary", "arbitrary", "arbitrary")),
        name="retention",
    )(lg, proj, proj, proj, proj, gn_gain.reshape(1, RET_V_W))


ATT_QB = 128


def _dil_attn_kernel(q_ref, km_ref, kp_ref, kn_ref, vm_ref, vp_ref, vn_ref, o_ref, lse_ref, *, L, Lq, d):
    i = pl.program_id(1)
    R, QB, HD = ATT_R, ATT_QB, ATT_HEAD_DIM
    nq = Lq // QB
    NK = QB + 2 * R
    row = lax.broadcasted_iota(I32, (QB, NK), 0)
    col = lax.broadcasted_iota(I32, (QB, NK), 1)
    band = jnp.logical_and(col - row >= 0, col - row <= 2 * R)
    lane = lax.broadcasted_iota(I32, (QB, LANES), 1)
    scale = HD ** -0.5

    def rows(ref, r, n):
        return ref[pl.ds(r, n, stride=d), :] if d > 1 else ref[...]

    def residue(r):
        q_all = rows(q_ref, r, Lq)
        k_all = jnp.concatenate([rows(kp_ref, r, R), rows(km_ref, r, Lq), rows(kn_ref, r, R)], axis=0)
        v_all = jnp.concatenate([rows(vp_ref, r, R), rows(vm_ref, r, Lq), rows(vn_ref, r, R)], axis=0)
        for qb in range(nq):
            base = i * Lq + qb * QB - R
            ok = jnp.logical_and(band, jnp.logical_and(col >= -base, col < L - base))
            q2 = _unpack_halves(q_all[qb * QB:(qb + 1) * QB])
            k2 = _unpack_halves(k_all[qb * QB:qb * QB + NK])
            v2 = _unpack_halves(v_all[qb * QB:qb * QB + NK])
            outs, lses = [], []
            for half in range(2):
                s = lax.dot_general(q2[half].astype(BF16), k2[half].astype(BF16), (((1,), (1,)), ((), ())),
                                    preferred_element_type=F32) * scale
                s = jnp.where(ok, s, NEG_BIG)
                m = jnp.max(s, axis=-1, keepdims=True)
                p = jnp.exp(s - m)
                den = jnp.sum(p, axis=-1, keepdims=True)
                pn = (p * (1.0 / den)).astype(BF16)
                outs.append(jnp.dot(pn, v2[half].astype(BF16), preferred_element_type=F32))
                lses.append(m + jnp.log(den))
            packed = _pack_halves(jnp.concatenate(outs, axis=1))
            lse_tile = jnp.where(lane < LANES // 2, lses[0], lses[1])
            dst = pl.ds(r + qb * QB * d, QB, stride=d) if d > 1 else pl.ds(qb * QB, QB)
            o_ref[dst, :] = packed
            lse_ref[dst, :] = lse_tile

    if d > 1:
        def body(r, carry):
            residue(r)
            return carry

        lax.fori_loop(0, d, body, 0)
    else:
        residue(0)


def _dil_attn(attp, g, B, S):
    window, d = ATT_PATTERNS[g]
    L = S // d
    Lq = min(L, 512 if d == 1 else 256)
    tb = Lq * d
    nb = S // tb
    halo = ATT_R * d
    hb = tb // halo
    nhb = S // halo
    npair = ATTP_BLK // LANES
    cq, ck, cv = (npair * (3 * g + kind) for kind in range(3))

    def main(c):
        return pl.BlockSpec((tb, LANES), lambda b, i, p: (b * nb + i, c + p))

    def prev(c):
        return pl.BlockSpec((halo, LANES), lambda b, i, p: (b * nhb + jnp.maximum(i * hb - 1, 0), c + p))

    def nxt(c):
        return pl.BlockSpec((halo, LANES), lambda b, i, p: (b * nhb + jnp.minimum((i + 1) * hb, nhb - 1), c + p))

    out_spec = pl.BlockSpec((tb, LANES), lambda b, i, p: (b * nb + i, p))
    return pl.pallas_call(
        functools.partial(_dil_attn_kernel, L=L, Lq=Lq, d=d),
        grid=(B, nb, npair),
        in_specs=[main(cq), main(ck), prev(ck), nxt(ck), main(cv), prev(cv), nxt(cv)],
        out_specs=[out_spec, out_spec],
        out_shape=[
            jax.ShapeDtypeStruct((B * S, npair * LANES), U32),
            jax.ShapeDtypeStruct((B * S, npair * LANES), F32),
        ],
        compiler_params=_cparams(("arbitrary", "arbitrary", "arbitrary")),
        name=f"dil_attn_{g}",
    )(attp, attp, attp, attp, attp, attp, attp)


MERGE_TILE = 512


def _merge_kernel(x_ref, r_ref, o0_ref, o1_ref, o2_ref, l0_ref, l1_ref, l2_ref, su_ref, sv_ref,
                  g0_ref, g1_ref, g2_ref, wr_ref, wa_ref, ws_ref, wo_ref, sgw_ref, sgb_ref, lng_ref,
                  nf_ref, wrt_ref, brt_ref, tri_ref,
                  x2_ref, hp_ref, route_ref, cnt_ref, carry_ref):
    i = pl.program_id(0)
    tm = x_ref.shape[0]
    HD = ATT_HEAD_DIM

    @pl.when(i == 0)
    def _():
        carry_ref[...] = jnp.zeros_like(carry_ref)

    l0, l1, l2 = l0_ref[...], l1_ref[...], l2_ref[...]
    m = jnp.maximum(jnp.maximum(l0, l1), l2)
    e0, e1, e2 = jnp.exp(l0 - m), jnp.exp(l1 - m), jnp.exp(l2 - m)
    inv = 1.0 / (e0 + e1 + e2)
    wts = (e0 * inv, e1 * inv, e2 * inv)
    outs = [jnp.concatenate(_unpack_halves(o_ref[...]), axis=1) for o_ref in (o0_ref, o1_ref, o2_ref)]
    heads = []
    for h in range(ATT_HEADS):
        cols = slice(h * HD, (h + 1) * HD)
        c = (h % 2) * LANES + (h // 2) * (LANES // 2)
        heads.append(sum(w[:, c:c + 1] * o[:, cols] for w, o in zip(wts, outs)))
    att = jnp.concatenate(heads, axis=1).astype(BF16)
    y_att = jnp.dot(att, wa_ref[...], preferred_element_type=F32)
    y_ret = jnp.dot(r_ref[...], wr_ref[...], preferred_element_type=F32)

    u = su_ref[...].astype(F32)
    v = sv_ref[...].astype(F32)
    mu = jnp.mean(v, axis=-1, keepdims=True)
    var = jnp.mean(jnp.square(v - mu), axis=-1, keepdims=True)
    vn = ((v - mu) * lax.rsqrt(var + NORM_EPS) * lng_ref[...]).astype(BF16)
    gw = SG_WIDTH // SG_GROUPS
    chunks = []
    for c in range(tm // SG_CHUNK):
        rows = slice(c * SG_CHUNK, (c + 1) * SG_CHUNK)
        parts = [jnp.dot(sgw_ref[g], vn[rows, g * gw:(g + 1) * gw], preferred_element_type=F32)
                 for g in range(SG_GROUPS)]
        chunks.append(jnp.concatenate(parts, axis=1) + sgb_ref[...])
    vs = jnp.concatenate(chunks, axis=0)
    y_sg = jnp.dot((u * vs).astype(BF16), ws_ref[...], preferred_element_type=F32)

    merged = (g0_ref[...].astype(F32) * y_ret + g1_ref[...].astype(F32) * y_att
              + g2_ref[...].astype(F32) * y_sg)
    x2 = x_ref[...] + jnp.dot(merged.astype(BF16), wo_ref[...], preferred_element_type=F32)
    x2_ref[...] = x2

    ms = jnp.mean(x2 * x2, axis=-1, keepdims=True)
    h2 = x2 * lax.rsqrt(ms + NORM_EPS) * nf_ref[...]
    hp_ref[...] = _pack_halves(h2)

    logits = jnp.dot(h2, wrt_ref[...], preferred_element_type=F32, precision=lax.Precision.HIGHEST) + brt_ref[...]
    lane = lax.broadcasted_iota(I32, logits.shape, 1)
    big = jnp.int32(LANES)
    is_g = lane < N_GROUPS
    gl = jnp.where(is_g, logits, -jnp.inf)
    gmax = jnp.max(gl, axis=-1, keepdims=True)
    g_idx = jnp.min(jnp.where(jnp.logical_and(is_g, gl == gmax), lane, big), axis=-1, keepdims=True)
    g_w = 1.0 / jnp.sum(jnp.where(is_g, jnp.exp(gl - gmax), 0.0), axis=-1, keepdims=True)
    lo = ROUTE_LANE0 + g_idx * EXP_PER_GROUP
    in_grp = jnp.logical_and(lane >= lo, lane < lo + EXP_PER_GROUP)
    el = jnp.where(in_grp, logits, -jnp.inf)
    v1 = jnp.max(el, axis=-1, keepdims=True)
    i1 = jnp.min(jnp.where(el == v1, lane, big), axis=-1, keepdims=True)
    el2 = jnp.where(lane == i1, -jnp.inf, el)
    v2 = jnp.max(el2, axis=-1, keepdims=True)
    i2 = jnp.min(jnp.where(el2 == v2, lane, big), axis=-1, keepdims=True)
    t = jnp.exp(v2 - v1)
    w1 = g_w / (1.0 + t)
    w2 = g_w * t / (1.0 + t)

    oh1 = lane == i1
    oh2 = lane == i2
    both = jnp.where(jnp.logical_or(oh1, oh2), 1.0, 0.0)
    before = jnp.dot(tri_ref[...], both.astype(BF16), preferred_element_type=F32) + carry_ref[0:1, :]
    rank1 = jnp.sum(jnp.where(oh1, before, 0.0), axis=-1, keepdims=True)
    rank2 = jnp.sum(jnp.where(oh2, before, 0.0), axis=-1, keepdims=True)
    carry = carry_ref[0:1, :] + jnp.sum(both, axis=0, keepdims=True)
    carry_ref[...] = jnp.broadcast_to(carry, carry_ref.shape)
    cnt_ref[...] = jnp.broadcast_to(carry, cnt_ref.shape)

    out = jnp.where(lane == 0, (i1 - ROUTE_LANE0).astype(F32), 0.0)
    out = jnp.where(lane == 1, (i2 - ROUTE_LANE0).astype(F32), out)
    out = jnp.where(lane == 2, w1, out)
    out = jnp.where(lane == 3, w2, out)
    out = jnp.where(lane == 4, rank1, out)
    out = jnp.where(lane == 5, rank2, out)
    route_ref[...] = out


def _merge(x2d, r, os_, lses, proj, lw):
    T = x2d.shape[0]
    tm = MERGE_TILE
    row = lambda w: pl.BlockSpec((tm, w), lambda i: (i, 0))
    pcol = lambda w, cb: pl.BlockSpec((tm, w), lambda i: (i, cb))
    full = lambda a: pl.BlockSpec(a.shape, lambda i: (0,) * a.ndim)
    gate_cb = CB_GATE * COL_BLK // D_MODEL
    weights = [lw["w_ret_out"], lw["w_att_out"], lw["w_sg_out"], lw["w_o"], lw["sg_w"], lw["sg_b_full"],
               lw["sg_ln_gain"], lw["norm_ffn"], lw["w_router"], lw["b_router"], lw["tri"]]
    return pl.pallas_call(
        _merge_kernel,
        grid=(T // tm,),
        in_specs=[row(D_MODEL), row(RET_V_W)] + [row(2 * LANES)] * (2 * len(ATT_PATTERNS)) + [
                  pcol(SG_WIDTH, CB_SGU), pcol(SG_WIDTH, CB_SGV),
                  pcol(D_MODEL, gate_cb), pcol(D_MODEL, gate_cb + 1), pcol(D_MODEL, gate_cb + 2)
                 ] + [full(a) for a in weights],
        out_specs=[row(D_MODEL), row(D_MODEL // 2), row(LANES), pl.BlockSpec((8, LANES), lambda i: (0, 0))],
        out_shape=[jax.ShapeDtypeStruct((T, D_MODEL), F32), jax.ShapeDtypeStruct((T, D_MODEL // 2), U32),
                   jax.ShapeDtypeStruct((T, LANES), F32), jax.ShapeDtypeStruct((8, LANES), F32)],
        scratch_shapes=[pltpu.VMEM((8, LANES), F32)],
        compiler_params=_cparams(("arbitrary",)),
        name="merge",
    )(x2d, r, *os_, *lses, proj, proj, proj, proj, proj, *weights)


DISPATCH_TILE = 1024


def _dispatch_kernel(dest_ref, h_ref, xb_in_ref, xb_ref, sem):
    del xb_in_ref
    tm = DISPATCH_TILE

    def issue(t, carry):
        src = h_ref.at[pl.ds(t, 1)]
        for k in range(TOP_K):
            pltpu.make_async_copy(src, xb_ref.at[pl.ds(dest_ref[0, 0, TOP_K * t + k], 1)], sem).start()
        return carry

    lax.fori_loop(0, tm, issue, 0)
    pltpu.make_async_copy(xb_ref.at[pl.ds(0, TOP_K * tm)], xb_ref.at[pl.ds(0, TOP_K * tm)], sem).wait()


def _dispatch(hp, dest, P):
    T = hp.shape[0]
    tm = DISPATCH_TILE
    half = D_MODEL // 2
    return pl.pallas_call(
        _dispatch_kernel,
        grid=(T // tm,),
        in_specs=[
            pl.BlockSpec((1, 1, TOP_K * tm), lambda i: (i, 0, 0), memory_space=pltpu.SMEM),
            pl.BlockSpec((tm, half), lambda i: (i, 0)),
            pl.BlockSpec(memory_space=pl.ANY),
        ],
        out_specs=pl.BlockSpec(memory_space=pl.ANY),
        out_shape=jax.ShapeDtypeStruct((P, half), U32),
        scratch_shapes=[pltpu.SemaphoreType.DMA(())],
        input_output_aliases={2: 0},
        compiler_params=_cparams(("arbitrary",)),
        name="dispatch",
    )(dest.reshape(T // tm, 1, TOP_K * tm), hp, jnp.zeros((P, half), U32))


def _experts_kernel(blk_e_ref, nused_ref, x_ref, wg_ref, wu_ref, wd_ref, y_ref):
    i = pl.program_id(0)

    @pl.when(i < nused_ref[0])
    def _():
        x = jnp.concatenate(_unpack_halves(x_ref[...]), axis=1).astype(BF16)
        gate = jnp.dot(x, wg_ref[0], preferred_element_type=F32)
        up = jnp.dot(x, wu_ref[0], preferred_element_type=F32)
        hid = (gate * _sigmoid(gate) * up).astype(BF16)
        y_ref[...] = jnp.dot(hid, wd_ref[0], preferred_element_type=F32)

    @pl.when(i >= nused_ref[0])
    def _():
        y_ref[...] = jnp.zeros_like(y_ref)


def _experts(xb, blk_e, nused, wg, wu, wd):
    P = xb.shape[0]
    bm = MOE_BLOCK
    grid_spec = pltpu.PrefetchScalarGridSpec(
        num_scalar_prefetch=2,
        grid=(P // bm,),
        in_specs=[
            pl.BlockSpec((bm, D_MODEL // 2), lambda i, e, n: (i, 0)),
            pl.BlockSpec((1, D_MODEL, EXP_HIDDEN), lambda i, e, n: (e[i], 0, 0)),
            pl.BlockSpec((1, D_MODEL, EXP_HIDDEN), lambda i, e, n: (e[i], 0, 0)),
            pl.BlockSpec((1, EXP_HIDDEN, D_MODEL), lambda i, e, n: (e[i], 0, 0)),
        ],
        out_specs=pl.BlockSpec((bm, D_MODEL), lambda i, e, n: (i, 0)),
    )
    return pl.pallas_call(
        _experts_kernel,
        grid_spec=grid_spec,
        out_shape=jax.ShapeDtypeStruct((P, D_MODEL), F32),
        compiler_params=_cparams(("arbitrary",)),
        name="experts",
    )(blk_e, nused, xb, wg, wu, wd)


COMBINE_TILE = 256


def _combine_kernel(dest_ref, x_ref, route_ref, gain_ref, y_ref, o_ref, buf_ref, sem, *, final_norm):
    tm = COMBINE_TILE

    def issue(t, carry):
        for k in range(TOP_K):
            pltpu.make_async_copy(y_ref.at[pl.ds(dest_ref[0, 0, TOP_K * t + k], 1)],
                                  buf_ref.at[k, pl.ds(t, 1)], sem).start()
        return carry

    lax.fori_loop(0, tm, issue, 0)
    pltpu.make_async_copy(buf_ref, buf_ref, sem).wait()
    route = route_ref[...]
    x = x_ref[...] + route[:, 2:3] * buf_ref[0] + route[:, 3:4] * buf_ref[1]
    if final_norm:
        ms = jnp.mean(x * x, axis=-1, keepdims=True)
        x = x * lax.rsqrt(ms + NORM_EPS) * gain_ref[...]
    o_ref[...] = x


def _combine(x2, route, yb, dest, gain, final_norm):
    T = x2.shape[0]
    tm = COMBINE_TILE
    return pl.pallas_call(
        functools.partial(_combine_kernel, final_norm=final_norm),
        grid=(T // tm,),
        in_specs=[
            pl.BlockSpec((1, 1, TOP_K * tm), lambda i: (i, 0, 0), memory_space=pltpu.SMEM),
            pl.BlockSpec((tm, D_MODEL), lambda i: (i, 0)),
            pl.BlockSpec((tm, LANES), lambda i: (i, 0)),
            pl.BlockSpec((1, D_MODEL), lambda i: (0, 0)),
            pl.BlockSpec(memory_space=pl.ANY),
        ],
        out_specs=pl.BlockSpec((tm, D_MODEL), lambda i: (i, 0)),
        out_shape=jax.ShapeDtypeStruct((T, D_MODEL), F32),
        scratch_shapes=[pltpu.VMEM((TOP_K, tm, D_MODEL), F32), pltpu.SemaphoreType.DMA(())],
        compiler_params=_cparams(("arbitrary",)),
        name="combine_final" if final_norm else "combine",
    )(dest.reshape(T // tm, 1, TOP_K * tm), x2, route, gain.reshape(1, D_MODEL), yb)


def _layer_weights(l, p):
    w_router = jnp.zeros((D_MODEL, LANES), F32)
    w_router = w_router.at[:, :N_GROUPS].set(p["w_router_group"][l])
    w_router = w_router.at[:, ROUTE_LANE0:ROUTE_LANE0 + N_EXPERTS].set(p["w_router_expert"][l])
    b_router = jnp.zeros((1, LANES), F32)
    b_router = b_router.at[0, :N_GROUPS].set(p["b_router_group"][l])
    b_router = b_router.at[0, ROUTE_LANE0:ROUTE_LANE0 + N_EXPERTS].set(p["b_router_expert"][l])
    tm = MERGE_TILE
    tri = (jnp.arange(tm)[:, None] > jnp.arange(tm)[None, :]).astype(BF16)
    gw = SG_WIDTH // SG_GROUPS
    w_in, w_in_rot = _in_proj_weights(p["w_in"][l])
    return {
        "norm_mix": p["norm_mix"][l],
        "w_in": w_in,
        "w_in_rot": w_in_rot,
        "lg": jnp.stack([-jnp.exp(p["ret_decay_fwd"][l].astype(F32)), -jnp.exp(p["ret_decay_bwd"][l].astype(F32))]),
        "ret_gn_gain": p["ret_gn_gain"][l],
        "w_ret_out": p["w_ret_out"][l].astype(BF16),
        "w_att_out": p["w_att_out"][l].astype(BF16),
        "w_sg_out": p["w_sg_out"][l].astype(BF16),
        "w_o": p["w_o"][l].astype(BF16),
        "sg_w": p["sg_w"][l].astype(BF16),
        "sg_b_full": jnp.repeat(p["sg_b"][l].T, gw, axis=1),
        "sg_ln_gain": p["sg_ln_gain"][l].reshape(1, SG_WIDTH),
        "norm_ffn": p["norm_ffn"][l].reshape(1, D_MODEL),
        "w_router": w_router,
        "b_router": b_router,
        "tri": tri,
        "w_exp_gate": p["w_exp_gate"][l].astype(BF16),
        "w_exp_up": p["w_exp_up"][l].astype(BF16),
        "w_exp_down": p["w_exp_down"][l].astype(BF16),
    }


def _moe_plan(route, cnt, T):
    bm = MOE_BLOCK
    eid = route[:, 0:TOP_K].astype(I32)
    rank = route[:, 4:4 + TOP_K].astype(I32)
    counts = cnt[0, ROUTE_LANE0:ROUTE_LANE0 + N_EXPERTS].astype(I32)
    padded = (counts + bm - 1) // bm * bm
    pad_end = jnp.cumsum(padded)
    pad_start = pad_end - padded
    dest = pad_start[eid] + rank
    P = T * TOP_K + N_EXPERTS * bm
    nblk = P // bm
    blk_e = jnp.minimum(jnp.searchsorted(pad_end, jnp.arange(nblk, dtype=I32) * bm, side="right"),
                        N_EXPERTS - 1).astype(I32)
    nused = (pad_end[-1:] // bm).astype(I32)
    return dest.reshape(-1), blk_e, nused, P


def _run_trunk(x, lws, tabs, norm_final):
    B, S, _ = x.shape
    T = B * S
    x2d = x.reshape(T, D_MODEL)
    tm_in = min(1024, S)
    for l, lw in enumerate(lws):
        proj, attp = _in_proj(x2d, lw["norm_mix"], lw["w_in"], lw["w_in_rot"], tabs, S, tm_in)
        r = _retention(proj, lw["lg"], lw["ret_gn_gain"], B, S)
        os_, lses = zip(*[_dil_attn(attp, g, B, S) for g in range(len(ATT_PATTERNS))])
        x2, hp, route, cnt = _merge(x2d, r, os_, lses, proj, lw)
        dest, blk_e, nused, P = _moe_plan(route, cnt, T)
        xb = _dispatch(hp, dest, P)
        yb = _experts(xb, blk_e, nused, lw["w_exp_gate"], lw["w_exp_up"], lw["w_exp_down"])
        x2d = _combine(x2, route, yb, dest, norm_final, final_norm=(l == len(lws) - 1))
    return x2d.reshape(B, S, D_MODEL)


def kernel(x_prompt, x_sample, norm_mix, w_in, ret_decay_fwd, ret_decay_bwd, ret_gn_gain, sg_ln_gain, sg_w, sg_b,
           w_ret_out, w_att_out, w_sg_out, w_o, norm_ffn, w_router_group, b_router_group, w_router_expert,
           b_router_expert, w_exp_gate, w_exp_up, w_exp_down, norm_final):
    p = dict(norm_mix=norm_mix, w_in=w_in, ret_decay_fwd=ret_decay_fwd, ret_decay_bwd=ret_decay_bwd,
             ret_gn_gain=ret_gn_gain, sg_ln_gain=sg_ln_gain, sg_w=sg_w, sg_b=sg_b, w_ret_out=w_ret_out,
             w_att_out=w_att_out, w_sg_out=w_sg_out, w_o=w_o, norm_ffn=norm_ffn, w_router_group=w_router_group,
             b_router_group=b_router_group, w_router_expert=w_router_expert, b_router_expert=b_router_expert,
             w_exp_gate=w_exp_gate, w_exp_up=w_exp_up, w_exp_down=w_exp_down)
    lws = [_layer_weights(l, p) for l in range(DEPTH)]
    s_max = max(x_prompt.shape[1], x_sample.shape[1])
    tabs = _rot_tables(s_max, RET_QK_DIM, RET_THETA, RET_QK_DIM) + _rot_tables(s_max, ATT_ROT_DIM, ROPE_THETA, ATT_HEAD_DIM)
    y_prompt = _run_trunk(x_prompt, lws, tabs, norm_final)
    y_sample = _run_trunk(x_sample, lws, tabs, norm_final)
    return (y_prompt, y_sample)
```

```python
import functools

import jax
import jax.numpy as jnp
import numpy as np
from jax import lax
from jax.experimental import pallas as pl
from jax.experimental.pallas import tpu as pltpu

F32 = jnp.float32
BF16 = jnp.bfloat16
U32 = jnp.uint32
I32 = jnp.int32

D_MODEL = 1024
DEPTH = 2
NORM_EPS = 1e-6
RET_HEADS = 4
RET_QK_DIM = 64
RET_V_DIM = 128
RET_CHUNK = 128
RET_THETA = 10000.0
ATT_PATTERNS = ((128, 1), (512, 4), (2048, 16))
ATT_HEADS = 4
ATT_HEAD_DIM = 128
ATT_ROT_DIM = ATT_HEAD_DIM // 4
ROPE_THETA = 500000.0
SG_GROUPS = 4
SG_CHUNK = 128
SG_WIDTH = 512
N_BRANCH = 3
N_GROUPS = 4
EXP_PER_GROUP = 8
N_EXPERTS = N_GROUPS * EXP_PER_GROUP
TOP_K = 2
EXP_HIDDEN = 512
RET_QK_W = RET_HEADS * RET_QK_DIM
RET_V_W = RET_HEADS * RET_V_DIM
ATT_W = ATT_HEADS * ATT_HEAD_DIM
ATT_IN_W = len(ATT_PATTERNS) * 3 * ATT_W
SG_IN_W = 2 * SG_WIDTH
GATE_IN_W = N_BRANCH * D_MODEL
N_IN = 2 * RET_QK_W + 2 * RET_V_W + ATT_IN_W + SG_IN_W + GATE_IN_W
NEG_BIG = -1e30

LANES = 128
COL_BLK = 512
N_COL_BLK = N_IN // COL_BLK
ATT_R = 64
assert all(w // (2 * d) == ATT_R for w, d in ATT_PATTERNS)
CB_GATE, CB_RV, CB_RG, CB_SGU, CB_SGV, CB_RQK, CB_ATT = 0, 6, 7, 8, 9, 10, 11
N_PROJ_BLK = CB_ATT
N_ATT_BLK = N_COL_BLK - CB_ATT
ATT_PAIR_SHIFT = LANES // 2
ATTP_BLK = COL_BLK // 2
ROUTE_LANE0 = N_GROUPS
MOE_BLOCK = 256
VMEM_LIMIT = 56 * 1024 * 1024
HI_HALF = 0xFFFF0000


def _cparams(sem, vmem=VMEM_LIMIT):
    return pltpu.CompilerParams(dimension_semantics=sem, vmem_limit_bytes=vmem)


def _sigmoid(x):
    return 0.5 * jnp.tanh(0.5 * x) + 0.5


def _pack_halves(y):
    n = y.shape[1] // 2
    bits = lax.bitcast_convert_type(y.astype(BF16).astype(F32), U32)
    return (bits[:, :n] >> 16) | (bits[:, n:] & jnp.uint32(HI_HALF))


def _unpack_halves(p):
    return (lax.bitcast_convert_type(p << 16, F32), lax.bitcast_convert_type(p & jnp.uint32(HI_HALF), F32))


def _in_proj_kernel(x_ref, g_ref, w_ref, wrot_ref, rc_ref, rs_ref, ac_ref, as_ref, o_ref, oa_ref, h_ref):
    j = pl.program_id(1)

    @pl.when(j == 0)
    def _():
        x = x_ref[...]
        ms = jnp.mean(x * x, axis=-1, keepdims=True)
        h_ref[...] = (x * lax.rsqrt(ms + NORM_EPS) * g_ref[...]).astype(BF16)

    def dot(w):
        return jnp.dot(h_ref[...], w[...], preferred_element_type=F32)

    reps = COL_BLK // LANES

    def tiled(tab_ref):
        return jnp.concatenate([tab_ref[...]] * reps, axis=1)

    att_kind = (j - CB_ATT) % 3

    @pl.when(j < CB_RV)
    def _():
        o_ref[...] = _sigmoid(dot(w_ref)).astype(BF16)

    @pl.when(j == CB_RV)
    def _():
        o_ref[...] = dot(w_ref).astype(BF16)

    @pl.when(j == CB_RG)
    def _():
        acc = dot(w_ref)
        o_ref[...] = (acc * _sigmoid(acc)).astype(BF16)

    @pl.when(jnp.logical_or(j == CB_SGU, j == CB_SGV))
    def _():
        o_ref[...] = jax.nn.gelu(dot(w_ref)).astype(BF16)

    @pl.when(j == CB_RQK)
    def _():
        y = dot(w_ref) * tiled(rc_ref) + dot(wrot_ref) * tiled(rs_ref)
        lane = lax.broadcasted_iota(I32, y.shape, 1)
        y = jnp.where(lane >= RET_QK_W, y * (RET_QK_DIM ** -0.5), y)
        o_ref[...] = y.astype(BF16)

    @pl.when(jnp.logical_and(j >= CB_ATT, att_kind != 2))
    def _():
        acc = dot(w_ref)
        partner = jnp.concatenate([pltpu.roll(acc[:, c * LANES:(c + 1) * LANES], ATT_PAIR_SHIFT, 1)
                                   for c in range(reps)], axis=1)
        oa_ref[...] = _pack_halves(acc * tiled(ac_ref) + partner * tiled(as_ref))

    @pl.when(jnp.logical_and(j >= CB_ATT, att_kind == 2))
    def _():
        oa_ref[...] = _pack_halves(dot(w_ref))


def _rot_angles(S, n_rot, theta):
    half = n_rot // 2
    inv = jnp.power(jnp.float32(theta), -jnp.arange(half, dtype=F32) * (2.0 / n_rot))
    ang = jnp.arange(S, dtype=F32)[:, None] * inv[None, :]
    return jnp.cos(ang), jnp.sin(ang)


def _ret_rot_tables(S):
    cos, sin = _rot_angles(S, RET_QK_DIM, RET_THETA)
    return jnp.tile(cos, (1, LANES // cos.shape[1])), jnp.tile(sin, (1, LANES // sin.shape[1]))


def _att_rot_tables(S):
    cos, sin = _rot_angles(S, ATT_ROT_DIM, ROPE_THETA)
    half = ATT_ROT_DIM // 2
    gap = jnp.ones((S, ATT_PAIR_SHIFT - half), F32)
    cos_h = jnp.concatenate([cos, gap, cos, gap], axis=1)
    sin_h = jnp.concatenate([-sin, 0.0 * gap, sin, 0.0 * gap], axis=1)
    return cos_h, sin_h


def _att_head_order(w):
    half = ATT_ROT_DIM // 2
    k, n = w.shape
    wh = w.reshape(k, n // ATT_HEAD_DIM, ATT_HEAD_DIM)
    s = ATT_PAIR_SHIFT
    out = jnp.concatenate([wh[..., :half], wh[..., s:s + half], wh[..., 2 * half:s], wh[..., half:2 * half],
                           wh[..., s + half:]], axis=-1)
    return out.reshape(k, n)


def _rot_partner_cols(w, n_rot, period):
    half = n_rot // 2
    k, n = w.shape
    wh = w.reshape(k, n // period, period)
    out = jnp.concatenate([-wh[..., half:n_rot], wh[..., :half], jnp.zeros_like(wh[..., n_rot:])], axis=-1)
    return out.reshape(k, n)


def _in_proj_weights(w_in):
    o = np.cumsum([0, RET_QK_W, RET_QK_W, RET_V_W, RET_V_W, ATT_IN_W, SG_IN_W, GATE_IN_W])
    rqk, rv, rg, att, sg, gate = (w_in[:, o[0]:o[2]], w_in[:, o[2]:o[3]], w_in[:, o[3]:o[4]], w_in[:, o[4]:o[5]],
                                  w_in[:, o[5]:o[6]], w_in[:, o[6]:o[7]])
    att_blocks = []
    for blk in range(N_ATT_BLK):
        cols = att[:, blk * ATT_W:(blk + 1) * ATT_W]
        att_blocks.append(cols if blk % 3 == 2 else _att_head_order(cols))
    w = jnp.concatenate([gate, rv, rg, sg, rqk] + att_blocks, axis=1)
    return w.astype(BF16), _rot_partner_cols(rqk, RET_QK_DIM, RET_QK_DIM).astype(BF16)


def _in_proj(x2d, gain, w_bf16, wrot_bf16, tabs, S, tm):
    T = x2d.shape[0]
    nrep = S // tm
    tab_spec = pl.BlockSpec((tm, LANES), lambda i, j: (i % nrep, 0))

    return pl.pallas_call(
        _in_proj_kernel,
        grid=(T // tm, N_COL_BLK),
        in_specs=[
            pl.BlockSpec((tm, D_MODEL), lambda i, j: (i, 0)),
            pl.BlockSpec((1, D_MODEL), lambda i, j: (0, 0)),
            pl.BlockSpec((D_MODEL, COL_BLK), lambda i, j: (0, j)),
            pl.BlockSpec((D_MODEL, COL_BLK), lambda i, j: (0, 0)),
            tab_spec, tab_spec, tab_spec, tab_spec,
        ],
        out_specs=[
            pl.BlockSpec((tm, COL_BLK), lambda i, j: (i, jnp.minimum(j, N_PROJ_BLK - 1))),
            pl.BlockSpec((tm, ATTP_BLK), lambda i, j: (i, jnp.maximum(j - CB_ATT, 0))),
        ],
        out_shape=[
            jax.ShapeDtypeStruct((T, N_PROJ_BLK * COL_BLK), BF16),
            jax.ShapeDtypeStruct((T, N_ATT_BLK * ATTP_BLK), U32),
        ],
        scratch_shapes=[pltpu.VMEM((tm, D_MODEL), BF16)],
        compiler_params=_cparams(("arbitrary", "arbitrary")),
        name="in_proj",
    )(x2d, gain.reshape(1, D_MODEL), w_bf16, wrot_bf16, *tabs)


RET_TILE = 512
RET_CPT = RET_TILE // RET_CHUNK


def _retention_kernel(lg_ref, q_ref, k_ref, v_ref, g_ref, gn_ref, o_ref,
                      dmat_ref, tab_ref, sf_ref, sbrun_ref, sb_ref, *, n_tiles):
    b, ph, i = pl.program_id(0), pl.program_id(1), pl.program_id(2)
    C, H, DK, DV = RET_CHUNK, RET_HEADS, RET_QK_DIM, RET_V_DIM

    @pl.when(jnp.logical_and(b == 0, jnp.logical_and(ph == 0, i == 0)))
    def _():
        row = lax.broadcasted_iota(I32, (C, C), 0).astype(F32)
        col = lax.broadcasted_iota(I32, (C, C), 1).astype(F32)
        diff = row - col
        for h in range(H):
            fwd = jnp.exp(jnp.maximum(diff, 0.0) * lg_ref[0, h])
            bwd = jnp.exp(jnp.maximum(-diff, 0.0) * lg_ref[1, h])
            dmat_ref[h] = jnp.where(diff >= 0, fwd, bwd)
        pos = lax.broadcasted_iota(I32, (C, RET_QK_W), 0).astype(F32)
        head = lax.broadcasted_iota(I32, (C, RET_QK_W), 1) >> 6
        lgf = jnp.zeros((C, RET_QK_W), F32)
        lgb = jnp.zeros((C, RET_QK_W), F32)
        for h in range(H):
            lgf = jnp.where(head == h, lg_ref[0, h], lgf)
            lgb = jnp.where(head == h, lg_ref[1, h], lgb)
        tab_ref[0] = jnp.exp((C - 1.0 - pos) * lgf)
        tab_ref[1] = jnp.exp((pos + 1.0) * lgf)
        tab_ref[2] = jnp.exp(pos * lgb)
        tab_ref[3] = jnp.exp((C - pos) * lgb)
        tab_ref[4] = jnp.exp(C * lgf)
        tab_ref[5] = jnp.exp(C * lgb)

    def chunk_decay(idx):
        return tab_ref[idx][:DV, :].T

    def kv_state(kd, v):
        kdt = kd.T.astype(BF16)
        parts = [jnp.dot(kdt[h * DK:(h + 1) * DK, :], v[:, h * DV:(h + 1) * DV], preferred_element_type=F32)
                 for h in range(H)]
        return jnp.concatenate(parts, axis=0)

    @pl.when(ph == 0)
    def _():
        @pl.when(i == 0)
        def _():
            sbrun_ref[...] = jnp.zeros_like(sbrun_ref)

        tile = n_tiles - 1 - i
        dec = chunk_decay(5)
        for c in reversed(range(RET_CPT)):
            rows = pl.ds(c * C, C)
            sb_ref[tile * RET_CPT + c] = sbrun_ref[...]
            kd = k_ref[rows, :].astype(F32) * tab_ref[2]
            sbrun_ref[...] = dec * sbrun_ref[...] + kv_state(kd, v_ref[rows, :])

    @pl.when(ph == 1)
    def _():
        @pl.when(i == 0)
        def _():
            sf_ref[...] = jnp.zeros_like(sf_ref)

        dec = chunk_decay(4)
        head = lax.broadcasted_iota(I32, (C, RET_QK_W), 1) >> 6
        for c in range(RET_CPT):
            rows = pl.ds(c * C, C)
            q = q_ref[rows, :].astype(F32)
            k = k_ref[rows, :].astype(F32)
            v = v_ref[rows, :]
            kt = k.T.astype(BF16)
            qf = q * tab_ref[1]
            qb = q * tab_ref[3]
            sf = sf_ref[...].astype(BF16)
            sb = sb_ref[i * RET_CPT + c].astype(BF16)
            outs = []
            for h in range(H):
                sel = head == h
                qh = jnp.where(sel, q, 0.0).astype(BF16)
                s = jnp.dot(qh, kt, preferred_element_type=F32) * dmat_ref[h]
                o = jnp.dot(s.astype(BF16), v[:, h * DV:(h + 1) * DV], preferred_element_type=F32)
                o += jnp.dot(jnp.where(sel, qf, 0.0).astype(BF16), sf, preferred_element_type=F32)
                o += jnp.dot(jnp.where(sel, qb, 0.0).astype(BF16), sb, preferred_element_type=F32)
                mu = jnp.mean(o, axis=-1, keepdims=True)
                var = jnp.mean(jnp.square(o - mu), axis=-1, keepdims=True)
                outs.append((o - mu) * lax.rsqrt(var + NORM_EPS))
            y = jnp.concatenate(outs, axis=1) * gn_ref[...]
            o_ref[rows, :] = (y * g_ref[rows, :].astype(F32)).astype(BF16)
            sf_ref[...] = dec * sf_ref[...] + kv_state(k * tab_ref[0], v)


def _retention(proj, lg, gn_gain, B, S):
    nt = S // RET_TILE
    nc = S // RET_CHUNK
    qkw = RET_QK_W

    def kv_idx(blk):
        def f(b, ph, i, lg_ref):
            t = jnp.where(ph == 0, nt - 1 - i, i)
            return (b * nt + t, blk)
        return f

    def out_idx(blk):
        def f(b, ph, i, lg_ref):
            return (b * nt + i * ph, blk)
        return f

    grid_spec = pltpu.PrefetchScalarGridSpec(
        num_scalar_prefetch=1,
        grid=(B, 2, nt),
        in_specs=[
            pl.BlockSpec((RET_TILE, qkw), out_idx(2 * CB_RQK)),
            pl.BlockSpec((RET_TILE, qkw), kv_idx(2 * CB_RQK + 1)),
            pl.BlockSpec((RET_TILE, RET_V_W), kv_idx(CB_RV)),
            pl.BlockSpec((RET_TILE, RET_V_W), out_idx(CB_RG)),
            pl.BlockSpec((1, RET_V_W), lambda b, ph, i, lg_ref: (0, 0)),
        ],
        out_specs=pl.BlockSpec((RET_TILE, RET_V_W), out_idx(0)),
        scratch_shapes=[
            pltpu.VMEM((RET_HEADS, RET_CHUNK, RET_CHUNK), F32),
            pltpu.VMEM((6, RET_CHUNK, qkw), F32),
            pltpu.VMEM((qkw, RET_V_DIM), F32),
            pltpu.VMEM((qkw, RET_V_DIM), F32),
            pltpu.VMEM((nc, qkw, RET_V_DIM), F32),
        ],
    )
    return pl.pallas_call(
        functools.partial(_retention_kernel, n_tiles=nt),
        grid_spec=grid_spec,
        out_shape=jax.ShapeDtypeStruct((B * S, RET_V_W), BF16),
        compiler_params=_cparams(("arbitrary", "arbitrary", "arbitrary")),
        name="retention",
    )(lg, proj, proj, proj, proj, gn_gain.reshape(1, RET_V_W))


ATT_QB = 128


def _dil_attn_kernel(q_ref, km_ref, kp_ref, kn_ref, vm_ref, vp_ref, vn_ref, o_ref, lse_ref, *, L, Lq, d):
    i = pl.program_id(1)
    R, QB, HD = ATT_R, ATT_QB, ATT_HEAD_DIM
    nq = Lq // QB
    NK = QB + 2 * R
    row = lax.broadcasted_iota(I32, (QB, NK), 0)
    col = lax.broadcasted_iota(I32, (QB, NK), 1)
    band = jnp.logical_and(col - row >= 0, col - row <= 2 * R)
    lane = lax.broadcasted_iota(I32, (QB, LANES), 1)
    scale = HD ** -0.5

    def rows(ref, r, n):
        return ref[pl.ds(r, n, stride=d), :] if d > 1 else ref[...]

    def residue(r):
        q_all = rows(q_ref, r, Lq)
        k_all = jnp.concatenate([rows(kp_ref, r, R), rows(km_ref, r, Lq), rows(kn_ref, r, R)], axis=0)
        v_all = jnp.concatenate([rows(vp_ref, r, R), rows(vm_ref, r, Lq), rows(vn_ref, r, R)], axis=0)
        for qb in range(nq):
            base = i * Lq + qb * QB - R
            ok = jnp.logical_and(band, jnp.logical_and(col >= -base, col < L - base))
            q2 = _unpack_halves(q_all[qb * QB:(qb + 1) * QB])
            k2 = _unpack_halves(k_all[qb * QB:qb * QB + NK])
            v2 = _unpack_halves(v_all[qb * QB:qb * QB + NK])
            outs, lses = [], []
            for half in range(2):
                s = lax.dot_general(q2[half].astype(BF16), k2[half].astype(BF16), (((1,), (1,)), ((), ())),
                                    preferred_element_type=F32) * scale
                s = jnp.where(ok, s, NEG_BIG)
                m = jnp.max(s, axis=-1, keepdims=True)
                p = jnp.exp(s - m)
                den = jnp.sum(p, axis=-1, keepdims=True)
                pn = (p * (1.0 / den)).astype(BF16)
                outs.append(jnp.dot(pn, v2[half].astype(BF16), preferred_element_type=F32))
                lses.append(m + jnp.log(den))
            packed = _pack_halves(jnp.concatenate(outs, axis=1))
            lse_tile = jnp.where(lane < LANES // 2, lses[0], lses[1])
            dst = pl.ds(r + qb * QB * d, QB, stride=d) if d > 1 else pl.ds(qb * QB, QB)
            o_ref[dst, :] = packed
            lse_ref[dst, :] = lse_tile

    if d > 1:
        def body(r, carry):
            residue(r)
            return carry

        lax.fori_loop(0, d, body, 0)
    else:
        residue(0)


def _dil_attn(attp, g, B, S):
    window, d = ATT_PATTERNS[g]
    L = S // d
    Lq = min(L, 512 if d == 1 else 256)
    tb = Lq * d
    nb = S // tb
    halo = ATT_R * d
    hb = tb // halo
    nhb = S // halo
    npair = ATTP_BLK // LANES
    cq, ck, cv = (npair * (3 * g + kind) for kind in range(3))

    def main(c):
        return pl.BlockSpec((tb, LANES), lambda b, i, p: (b * nb + i, c + p))

    def prev(c):
        return pl.BlockSpec((halo, LANES), lambda b, i, p: (b * nhb + jnp.maximum(i * hb - 1, 0), c + p))

    def nxt(c):
        return pl.BlockSpec((halo, LANES), lambda b, i, p: (b * nhb + jnp.minimum((i + 1) * hb, nhb - 1), c + p))

    out_spec = pl.BlockSpec((tb, LANES), lambda b, i, p: (b * nb + i, p))
    return pl.pallas_call(
        functools.partial(_dil_attn_kernel, L=L, Lq=Lq, d=d),
        grid=(B, nb, npair),
        in_specs=[main(cq), main(ck), prev(ck), nxt(ck), main(cv), prev(cv), nxt(cv)],
        out_specs=[out_spec, out_spec],
        out_shape=[
            jax.ShapeDtypeStruct((B * S, npair * LANES), U32),
            jax.ShapeDtypeStruct((B * S, npair * LANES), F32),
        ],
        compiler_params=_cparams(("arbitrary", "arbitrary", "arbitrary")),
        name=f"dil_attn_{g}",
    )(attp, attp, attp, attp, attp, attp, attp)


MERGE_TILE = 512


def _merge_kernel(x_ref, r_ref, o0_ref, o1_ref, o2_ref, l0_ref, l1_ref, l2_ref, su_ref, sv_ref,
                  g0_ref, g1_ref, g2_ref, wr_ref, wa_ref, ws_ref, wo_ref, sgw_ref, sgb_ref, lng_ref,
                  nf_ref, wrt_ref, brt_ref, tri_ref,
                  x2_ref, hp_ref, route_ref, cnt_ref, carry_ref):
    i = pl.program_id(0)
    tm = x_ref.shape[0]
    HD = ATT_HEAD_DIM

    @pl.when(i == 0)
    def _():
        carry_ref[...] = jnp.zeros_like(carry_ref)

    l0, l1, l2 = l0_ref[...], l1_ref[...], l2_ref[...]
    m = jnp.maximum(jnp.maximum(l0, l1), l2)
    e0, e1, e2 = jnp.exp(l0 - m), jnp.exp(l1 - m), jnp.exp(l2 - m)
    inv = 1.0 / (e0 + e1 + e2)
    wts = (e0 * inv, e1 * inv, e2 * inv)
    outs = [jnp.concatenate(_unpack_halves(o_ref[...]), axis=1) for o_ref in (o0_ref, o1_ref, o2_ref)]
    heads = []
    for h in range(ATT_HEADS):
        cols = slice(h * HD, (h + 1) * HD)
        c = (h % 2) * LANES + (h // 2) * (LANES // 2)
        heads.append(sum(w[:, c:c + 1] * o[:, cols] for w, o in zip(wts, outs)))
    att = jnp.concatenate(heads, axis=1).astype(BF16)
    y_att = jnp.dot(att, wa_ref[...], preferred_element_type=F32)
    y_ret = jnp.dot(r_ref[...], wr_ref[...], preferred_element_type=F32)

    u = su_ref[...].astype(F32)
    v = sv_ref[...].astype(F32)
    mu = jnp.mean(v, axis=-1, keepdims=True)
    var = jnp.mean(jnp.square(v - mu), axis=-1, keepdims=True)
    vn = ((v - mu) * lax.rsqrt(var + NORM_EPS) * lng_ref[...]).astype(BF16)
    gw = SG_WIDTH // SG_GROUPS
    chunks = []
    for c in range(tm // SG_CHUNK):
        rows = slice(c * SG_CHUNK, (c + 1) * SG_CHUNK)
        parts = [jnp.dot(sgw_ref[g], vn[rows, g * gw:(g + 1) * gw], preferred_element_type=F32)
                 for g in range(SG_GROUPS)]
        chunks.append(jnp.concatenate(parts, axis=1) + sgb_ref[...])
    vs = jnp.concatenate(chunks, axis=0)
    y_sg = jnp.dot((u * vs).astype(BF16), ws_ref[...], preferred_element_type=F32)

    merged = (g0_ref[...].astype(F32) * y_ret + g1_ref[...].astype(F32) * y_att
              + g2_ref[...].astype(F32) * y_sg)
    x2 = x_ref[...] + jnp.dot(merged.astype(BF16), wo_ref[...], preferred_element_type=F32)
    x2_ref[...] = x2

    ms = jnp.mean(x2 * x2, axis=-1, keepdims=True)
    h2 = x2 * lax.rsqrt(ms + NORM_EPS) * nf_ref[...]
    hp_ref[...] = _pack_halves(h2)

    h_hi = h2.astype(BF16)
    h_lo = (h2 - h_hi.astype(F32)).astype(BF16)
    hw = jnp.dot(h_hi, wrt_ref[...], preferred_element_type=F32)
    logits = (hw[:, :LANES] + hw[:, LANES:] + jnp.dot(h_lo, wrt_ref[:, :LANES], preferred_element_type=F32)
              + brt_ref[...])
    lane = lax.broadcasted_iota(I32, logits.shape, 1)
    big = jnp.int32(LANES)
    is_g = lane < N_GROUPS
    gl = jnp.where(is_g, logits, -jnp.inf)
    gmax = jnp.max(gl, axis=-1, keepdims=True)
    g_idx = jnp.min(jnp.where(jnp.logical_and(is_g, gl == gmax), lane, big), axis=-1, keepdims=True)
    g_w = 1.0 / jnp.sum(jnp.where(is_g, jnp.exp(gl - gmax), 0.0), axis=-1, keepdims=True)
    lo = ROUTE_LANE0 + g_idx * EXP_PER_GROUP
    in_grp = jnp.logical_and(lane >= lo, lane < lo + EXP_PER_GROUP)
    el = jnp.where(in_grp, logits, -jnp.inf)
    v1 = jnp.max(el, axis=-1, keepdims=True)
    i1 = jnp.min(jnp.where(el == v1, lane, big), axis=-1, keepdims=True)
    el2 = jnp.where(lane == i1, -jnp.inf, el)
    v2 = jnp.max(el2, axis=-1, keepdims=True)
    i2 = jnp.min(jnp.where(el2 == v2, lane, big), axis=-1, keepdims=True)
    t = jnp.exp(v2 - v1)
    w1 = g_w / (1.0 + t)
    w2 = g_w * t / (1.0 + t)

    oh1 = lane == i1
    oh2 = lane == i2
    both = jnp.where(jnp.logical_or(oh1, oh2), 1.0, 0.0)
    before = jnp.dot(tri_ref[...], both.astype(BF16), preferred_element_type=F32) + carry_ref[0:1, :]
    rank1 = jnp.sum(jnp.where(oh1, before, 0.0), axis=-1, keepdims=True)
    rank2 = jnp.sum(jnp.where(oh2, before, 0.0), axis=-1, keepdims=True)
    carry = carry_ref[0:1, :] + jnp.sum(both, axis=0, keepdims=True)
    carry_ref[...] = jnp.broadcast_to(carry, carry_ref.shape)
    cnt_ref[...] = jnp.broadcast_to(carry, cnt_ref.shape)

    out = jnp.where(lane == 0, (i1 - ROUTE_LANE0).astype(F32), 0.0)
    out = jnp.where(lane == 1, (i2 - ROUTE_LANE0).astype(F32), out)
    out = jnp.where(lane == 2, w1, out)
    out = jnp.where(lane == 3, w2, out)
    out = jnp.where(lane == 4, rank1, out)
    out = jnp.where(lane == 5, rank2, out)
    route_ref[...] = out


def _merge(x2d, r, os_, lses, proj, lw):
    T = x2d.shape[0]
    tm = MERGE_TILE
    row = lambda w: pl.BlockSpec((tm, w), lambda i: (i, 0))
    pcol = lambda w, cb: pl.BlockSpec((tm, w), lambda i: (i, cb))
    full = lambda a: pl.BlockSpec(a.shape, lambda i: (0,) * a.ndim)
    gate_cb = CB_GATE * COL_BLK // D_MODEL
    weights = [lw["w_ret_out"], lw["w_att_out"], lw["w_sg_out"], lw["w_o"], lw["sg_w"], lw["sg_b_full"],
               lw["sg_ln_gain"], lw["norm_ffn"], lw["w_router"], lw["b_router"], lw["tri"]]
    return pl.pallas_call(
        _merge_kernel,
        grid=(T // tm,),
        in_specs=[row(D_MODEL), row(RET_V_W)] + [row(2 * LANES)] * (2 * len(ATT_PATTERNS)) + [
                  pcol(SG_WIDTH, CB_SGU), pcol(SG_WIDTH, CB_SGV),
                  pcol(D_MODEL, gate_cb), pcol(D_MODEL, gate_cb + 1), pcol(D_MODEL, gate_cb + 2)
                 ] + [full(a) for a in weights],
        out_specs=[row(D_MODEL), row(D_MODEL // 2), row(LANES), pl.BlockSpec((8, LANES), lambda i: (0, 0))],
        out_shape=[jax.ShapeDtypeStruct((T, D_MODEL), F32), jax.ShapeDtypeStruct((T, D_MODEL // 2), U32),
                   jax.ShapeDtypeStruct((T, LANES), F32), jax.ShapeDtypeStruct((8, LANES), F32)],
        scratch_shapes=[pltpu.VMEM((8, LANES), F32)],
        compiler_params=_cparams(("arbitrary",)),
        name="merge",
    )(x2d, r, *os_, *lses, proj, proj, proj, proj, proj, *weights)


DISPATCH_TILE = 1024


def _dispatch_kernel(dest_ref, h_ref, xb_in_ref, xb_ref, sem):
    del xb_in_ref
    tm = DISPATCH_TILE

    def issue(t, carry):
        src = h_ref.at[pl.ds(t, 1)]
        for k in range(TOP_K):
            pltpu.make_async_copy(src, xb_ref.at[pl.ds(dest_ref[0, 0, TOP_K * t + k], 1)], sem).start(priority=k)
        return carry

    lax.fori_loop(0, tm, issue, 0)
    pltpu.make_async_copy(xb_ref.at[pl.ds(0, TOP_K * tm)], xb_ref.at[pl.ds(0, TOP_K * tm)], sem).wait()


def _dispatch(hp, dest, P):
    T = hp.shape[0]
    tm = DISPATCH_TILE
    half = D_MODEL // 2
    return pl.pallas_call(
        _dispatch_kernel,
        grid=(T // tm,),
        in_specs=[
            pl.BlockSpec((1, 1, TOP_K * tm), lambda i: (i, 0, 0), memory_space=pltpu.SMEM),
            pl.BlockSpec((tm, half), lambda i: (i, 0)),
            pl.BlockSpec(memory_space=pl.ANY),
        ],
        out_specs=pl.BlockSpec(memory_space=pl.ANY),
        out_shape=jax.ShapeDtypeStruct((P, half), U32),
        scratch_shapes=[pltpu.SemaphoreType.DMA(())],
        input_output_aliases={2: 0},
        compiler_params=_cparams(("arbitrary",)),
        name="dispatch",
    )(dest.reshape(T // tm, 1, TOP_K * tm), hp, jnp.zeros((P, half), U32))


def _experts_kernel(blk_e_ref, nused_ref, x_ref, wg_ref, wu_ref, wd_ref, y_ref):
    i = pl.program_id(0)

    @pl.when(i < nused_ref[0])
    def _():
        x = jnp.concatenate(_unpack_halves(x_ref[...]), axis=1).astype(BF16)
        gate = jnp.dot(x, wg_ref[0], preferred_element_type=F32)
        up = jnp.dot(x, wu_ref[0], preferred_element_type=F32)
        hid = (gate * _sigmoid(gate) * up).astype(BF16)
        y_ref[...] = jnp.dot(hid, wd_ref[0], preferred_element_type=F32)

    @pl.when(i >= nused_ref[0])
    def _():
        y_ref[...] = jnp.zeros_like(y_ref)


def _experts(xb, blk_e, nused, wg, wu, wd):
    P = xb.shape[0]
    bm = MOE_BLOCK
    grid_spec = pltpu.PrefetchScalarGridSpec(
        num_scalar_prefetch=2,
        grid=(P // bm,),
        in_specs=[
            pl.BlockSpec((bm, D_MODEL // 2), lambda i, e, n: (i, 0)),
            pl.BlockSpec((1, D_MODEL, EXP_HIDDEN), lambda i, e, n: (e[i], 0, 0)),
            pl.BlockSpec((1, D_MODEL, EXP_HIDDEN), lambda i, e, n: (e[i], 0, 0)),
            pl.BlockSpec((1, EXP_HIDDEN, D_MODEL), lambda i, e, n: (e[i], 0, 0)),
        ],
        out_specs=pl.BlockSpec((bm, D_MODEL), lambda i, e, n: (i, 0)),
    )
    return pl.pallas_call(
        _experts_kernel,
        grid_spec=grid_spec,
        out_shape=jax.ShapeDtypeStruct((P, D_MODEL), F32),
        compiler_params=_cparams(("arbitrary",)),
        name="experts",
    )(blk_e, nused, xb, wg, wu, wd)


COMBINE_TILE = 256


def _combine_kernel(dest_ref, x_ref, route_ref, gain_ref, y_ref, o_ref, buf_ref, sem, *, final_norm):
    tm = COMBINE_TILE

    def issue(t, carry):
        for k in range(TOP_K):
            pltpu.make_async_copy(y_ref.at[pl.ds(dest_ref[0, 0, TOP_K * t + k], 1)],
                                  buf_ref.at[k, pl.ds(t, 1)], sem).start(priority=k)
        return carry

    lax.fori_loop(0, tm, issue, 0)
    pltpu.make_async_copy(buf_ref, buf_ref, sem).wait()
    route = route_ref[...]
    x = x_ref[...] + route[:, 2:3] * buf_ref[0] + route[:, 3:4] * buf_ref[1]
    if final_norm:
        ms = jnp.mean(x * x, axis=-1, keepdims=True)
        x = x * lax.rsqrt(ms + NORM_EPS) * gain_ref[...]
    o_ref[...] = x


def _combine(x2, route, yb, dest, gain, final_norm):
    T = x2.shape[0]
    tm = COMBINE_TILE
    return pl.pallas_call(
        functools.partial(_combine_kernel, final_norm=final_norm),
        grid=(T // tm,),
        in_specs=[
            pl.BlockSpec((1, 1, TOP_K * tm), lambda i: (i, 0, 0), memory_space=pltpu.SMEM),
            pl.BlockSpec((tm, D_MODEL), lambda i: (i, 0)),
            pl.BlockSpec((tm, LANES), lambda i: (i, 0)),
            pl.BlockSpec((1, D_MODEL), lambda i: (0, 0)),
            pl.BlockSpec(memory_space=pl.ANY),
        ],
        out_specs=pl.BlockSpec((tm, D_MODEL), lambda i: (i, 0)),
        out_shape=jax.ShapeDtypeStruct((T, D_MODEL), F32),
        scratch_shapes=[pltpu.VMEM((TOP_K, tm, D_MODEL), F32), pltpu.SemaphoreType.DMA(())],
        compiler_params=_cparams(("arbitrary",)),
        name="combine_final" if final_norm else "combine",
    )(dest.reshape(T // tm, 1, TOP_K * tm), x2, route, gain.reshape(1, D_MODEL), yb)


def _layer_weights(l, p):
    w_router = jnp.zeros((D_MODEL, LANES), F32)
    w_router = w_router.at[:, :N_GROUPS].set(p["w_router_group"][l])
    w_router = w_router.at[:, ROUTE_LANE0:ROUTE_LANE0 + N_EXPERTS].set(p["w_router_expert"][l])
    b_router = jnp.zeros((1, LANES), F32)
    b_router = b_router.at[0, :N_GROUPS].set(p["b_router_group"][l])
    b_router = b_router.at[0, ROUTE_LANE0:ROUTE_LANE0 + N_EXPERTS].set(p["b_router_expert"][l])
    tm = MERGE_TILE
    tri = (jnp.arange(tm)[:, None] > jnp.arange(tm)[None, :]).astype(BF16)
    gw = SG_WIDTH // SG_GROUPS
    w_in, w_in_rot = _in_proj_weights(p["w_in"][l])
    return {
        "norm_mix": p["norm_mix"][l],
        "w_in": w_in,
        "w_in_rot": w_in_rot,
        "lg": jnp.stack([-jnp.exp(p["ret_decay_fwd"][l].astype(F32)), -jnp.exp(p["ret_decay_bwd"][l].astype(F32))]),
        "ret_gn_gain": p["ret_gn_gain"][l],
        "w_ret_out": p["w_ret_out"][l].astype(BF16),
        "w_att_out": p["w_att_out"][l].astype(BF16),
        "w_sg_out": p["w_sg_out"][l].astype(BF16),
        "w_o": p["w_o"][l].astype(BF16),
        "sg_w": p["sg_w"][l].astype(BF16),
        "sg_b_full": jnp.repeat(p["sg_b"][l].T, gw, axis=1),
        "sg_ln_gain": p["sg_ln_gain"][l].reshape(1, SG_WIDTH),
        "norm_ffn": p["norm_ffn"][l].reshape(1, D_MODEL),
        "w_router": jnp.concatenate([w_router.astype(BF16),
                                     (w_router - w_router.astype(BF16).astype(F32)).astype(BF16)], axis=1),
        "b_router": b_router,
        "tri": tri,
        "w_exp_gate": p["w_exp_gate"][l].astype(BF16),
        "w_exp_up": p["w_exp_up"][l].astype(BF16),
        "w_exp_down": p["w_exp_down"][l].astype(BF16),
    }


def _moe_plan(route, cnt, T):
    bm = MOE_BLOCK
    eid = route[:, 0:TOP_K].astype(I32)
    rank = route[:, 4:4 + TOP_K].astype(I32)
    counts = cnt[0, ROUTE_LANE0:ROUTE_LANE0 + N_EXPERTS].astype(I32)
    padded = (counts + bm - 1) // bm * bm
    pad_end = jnp.cumsum(padded)
    pad_start = pad_end - padded
    experts = jnp.arange(N_EXPERTS, dtype=I32)
    dest = jnp.sum(jnp.where(eid[..., None] == experts, pad_start, 0), axis=-1) + rank
    P = T * TOP_K + N_EXPERTS * bm
    nblk = P // bm
    blk_start = jnp.arange(nblk, dtype=I32) * bm
    blk_e = jnp.minimum(jnp.sum((pad_end[None, :] <= blk_start[:, None]).astype(I32), axis=1), N_EXPERTS - 1)
    nused = (pad_end[-1:] // bm).astype(I32)
    return dest.reshape(-1), blk_e, nused, P


def _run_trunk(x, lws, tabs, norm_final):
    B, S, _ = x.shape
    T = B * S
    x2d = x.reshape(T, D_MODEL)
    tm_in = min(2048, S)
    for l, lw in enumerate(lws):
        proj, attp = _in_proj(x2d, lw["norm_mix"], lw["w_in"], lw["w_in_rot"], tabs, S, tm_in)
        r = _retention(proj, lw["lg"], lw["ret_gn_gain"], B, S)
        os_, lses = zip(*[_dil_attn(attp, g, B, S) for g in range(len(ATT_PATTERNS))])
        x2, hp, route, cnt = _merge(x2d, r, os_, lses, proj, lw)
        dest, blk_e, nused, P = _moe_plan(route, cnt, T)
        xb = _dispatch(hp, dest, P)
        yb = _experts(xb, blk_e, nused, lw["w_exp_gate"], lw["w_exp_up"], lw["w_exp_down"])
        x2d = _combine(x2, route, yb, dest, norm_final, final_norm=(l == len(lws) - 1))
    return x2d.reshape(B, S, D_MODEL)


def kernel(x_prompt, x_sample, norm_mix, w_in, ret_decay_fwd, ret_decay_bwd, ret_gn_gain, sg_ln_gain, sg_w, sg_b,
           w_ret_out, w_att_out, w_sg_out, w_o, norm_ffn, w_router_group, b_router_group, w_router_expert,
           b_router_expert, w_exp_gate, w_exp_up, w_exp_down, norm_final):
    p = dict(norm_mix=norm_mix, w_in=w_in, ret_decay_fwd=ret_decay_fwd, ret_decay_bwd=ret_decay_bwd,
             ret_gn_gain=ret_gn_gain, sg_ln_gain=sg_ln_gain, sg_w=sg_w, sg_b=sg_b, w_ret_out=w_ret_out,
             w_att_out=w_att_out, w_sg_out=w_sg_out, w_o=w_o, norm_ffn=norm_ffn, w_router_group=w_router_group,
             b_router_group=b_router_group, w_router_expert=w_router_expert, b_router_expert=b_router_expert,
             w_exp_gate=w_exp_gate, w_exp_up=w_exp_up, w_exp_down=w_exp_down)
    lws = [_layer_weights(l, p) for l in range(DEPTH)]
    s_max = max(x_prompt.shape[1], x_sample.shape[1])
    tabs = _ret_rot_tables(s_max) + _att_rot_tables(s_max)
    y_prompt = _run_trunk(x_prompt, lws, tabs, norm_final)
    y_sample = _run_trunk(x_sample, lws, tabs, norm_final)
    return (y_prompt, y_sample)
```

```python
import functools

import jax
import jax.numpy as jnp
import numpy as np
from jax import lax
from jax.experimental import pallas as pl
from jax.experimental.pallas import tpu as pltpu

F32 = jnp.float32
BF16 = jnp.bfloat16
U32 = jnp.uint32
I32 = jnp.int32

D_MODEL = 1024
DEPTH = 2
NORM_EPS = 1e-6
RET_HEADS = 4
RET_QK_DIM = 64
RET_V_DIM = 128
RET_CHUNK = 128
RET_THETA = 10000.0
ATT_PATTERNS = ((128, 1), (512, 4), (2048, 16))
ATT_HEADS = 4
ATT_HEAD_DIM = 128
ATT_ROT_DIM = ATT_HEAD_DIM // 4
ROPE_THETA = 500000.0
SG_GROUPS = 4
SG_CHUNK = 128
SG_WIDTH = 512
N_BRANCH = 3
N_GROUPS = 4
EXP_PER_GROUP = 8
N_EXPERTS = N_GROUPS * EXP_PER_GROUP
TOP_K = 2
EXP_HIDDEN = 512
RET_QK_W = RET_HEADS * RET_QK_DIM
RET_V_W = RET_HEADS * RET_V_DIM
ATT_W = ATT_HEADS * ATT_HEAD_DIM
ATT_IN_W = len(ATT_PATTERNS) * 3 * ATT_W
SG_IN_W = 2 * SG_WIDTH
GATE_IN_W = N_BRANCH * D_MODEL
N_IN = 2 * RET_QK_W + 2 * RET_V_W + ATT_IN_W + SG_IN_W + GATE_IN_W
NEG_BIG = -1e30

LANES = 128
COL_BLK = 512
N_COL_BLK = N_IN // COL_BLK
ATT_R = 64
assert all(w // (2 * d) == ATT_R for w, d in ATT_PATTERNS)
CB_GATE, CB_RV, CB_RG, CB_SGU, CB_SGV, CB_RQK, CB_ATT = 0, 6, 7, 8, 9, 10, 11
N_PROJ_BLK = CB_ATT
N_ATT_BLK = N_COL_BLK - CB_ATT
ATT_PAIR_SHIFT = LANES // 2
ATTP_BLK = COL_BLK // 2
ROUTE_LANE0 = N_GROUPS
MOE_BLOCK = 256
VMEM_LIMIT = 56 * 1024 * 1024
HI_HALF = 0xFFFF0000


def _cparams(sem, vmem=VMEM_LIMIT):
    return pltpu.CompilerParams(dimension_semantics=sem, vmem_limit_bytes=vmem)


def _sigmoid(x):
    return 0.5 * jnp.tanh(0.5 * x) + 0.5


def _pack_halves(y):
    n = y.shape[1] // 2
    bits = lax.bitcast_convert_type(y.astype(BF16).astype(F32), U32)
    return (bits[:, :n] >> 16) | (bits[:, n:] & jnp.uint32(HI_HALF))


def _unpack_halves(p):
    return (lax.bitcast_convert_type(p << 16, F32), lax.bitcast_convert_type(p & jnp.uint32(HI_HALF), F32))


def _in_proj_kernel(x_ref, g_ref, w_ref, wrot_ref, rc_ref, rs_ref, ac_ref, as_ref, o_ref, oa_ref, h_ref):
    j = pl.program_id(1)

    @pl.when(j == 0)
    def _():
        x = x_ref[...]
        ms = jnp.mean(x * x, axis=-1, keepdims=True)
        h_ref[...] = (x * lax.rsqrt(ms + NORM_EPS) * g_ref[...]).astype(BF16)

    def dot(w):
        return jnp.dot(h_ref[...], w[...], preferred_element_type=F32)

    reps = COL_BLK // LANES

    def tiled(tab_ref):
        return jnp.concatenate([tab_ref[...]] * reps, axis=1)

    att_kind = (j - CB_ATT) % 3

    @pl.when(j < CB_RV)
    def _():
        o_ref[...] = _sigmoid(dot(w_ref)).astype(BF16)

    @pl.when(j == CB_RV)
    def _():
        o_ref[...] = dot(w_ref).astype(BF16)

    @pl.when(j == CB_RG)
    def _():
        acc = dot(w_ref)
        o_ref[...] = (acc * _sigmoid(acc)).astype(BF16)

    @pl.when(jnp.logical_or(j == CB_SGU, j == CB_SGV))
    def _():
        o_ref[...] = jax.nn.gelu(dot(w_ref)).astype(BF16)

    @pl.when(j == CB_RQK)
    def _():
        y = dot(w_ref) * tiled(rc_ref) + dot(wrot_ref) * tiled(rs_ref)
        lane = lax.broadcasted_iota(I32, y.shape, 1)
        y = jnp.where(lane >= RET_QK_W, y * (RET_QK_DIM ** -0.5), y)
        o_ref[...] = y.astype(BF16)

    @pl.when(jnp.logical_and(j >= CB_ATT, att_kind != 2))
    def _():
        acc = dot(w_ref)
        partner = jnp.concatenate([pltpu.roll(acc[:, c * LANES:(c + 1) * LANES], ATT_PAIR_SHIFT, 1)
                                   for c in range(reps)], axis=1)
        oa_ref[...] = _pack_halves(acc * tiled(ac_ref) + partner * tiled(as_ref))

    @pl.when(jnp.logical_and(j >= CB_ATT, att_kind == 2))
    def _():
        oa_ref[...] = _pack_halves(dot(w_ref))


def _rot_angles(S, n_rot, theta):
    half = n_rot // 2
    inv = jnp.power(jnp.float32(theta), -jnp.arange(half, dtype=F32) * (2.0 / n_rot))
    ang = jnp.arange(S, dtype=F32)[:, None] * inv[None, :]
    return jnp.cos(ang), jnp.sin(ang)


def _ret_rot_tables(S):
    cos, sin = _rot_angles(S, RET_QK_DIM, RET_THETA)
    return jnp.tile(cos, (1, LANES // cos.shape[1])), jnp.tile(sin, (1, LANES // sin.shape[1]))


def _att_rot_tables(S):
    cos, sin = _rot_angles(S, ATT_ROT_DIM, ROPE_THETA)
    half = ATT_ROT_DIM // 2
    gap = jnp.ones((S, ATT_PAIR_SHIFT - half), F32)
    cos_h = jnp.concatenate([cos, gap, cos, gap], axis=1)
    sin_h = jnp.concatenate([-sin, 0.0 * gap, sin, 0.0 * gap], axis=1)
    return cos_h, sin_h


def _att_head_order(w):
    half = ATT_ROT_DIM // 2
    k, n = w.shape
    wh = w.reshape(k, n // ATT_HEAD_DIM, ATT_HEAD_DIM)
    s = ATT_PAIR_SHIFT
    out = jnp.concatenate([wh[..., :half], wh[..., s:s + half], wh[..., 2 * half:s], wh[..., half:2 * half],
                           wh[..., s + half:]], axis=-1)
    return out.reshape(k, n)


def _rot_partner_cols(w, n_rot, period):
    half = n_rot // 2
    k, n = w.shape
    wh = w.reshape(k, n // period, period)
    out = jnp.concatenate([-wh[..., half:n_rot], wh[..., :half], jnp.zeros_like(wh[..., n_rot:])], axis=-1)
    return out.reshape(k, n)


def _in_proj_weights(w_in):
    o = np.cumsum([0, RET_QK_W, RET_QK_W, RET_V_W, RET_V_W, ATT_IN_W, SG_IN_W, GATE_IN_W])
    rqk, rv, rg, att, sg, gate = (w_in[:, o[0]:o[2]], w_in[:, o[2]:o[3]], w_in[:, o[3]:o[4]], w_in[:, o[4]:o[5]],
                                  w_in[:, o[5]:o[6]], w_in[:, o[6]:o[7]])
    att_blocks = []
    for blk in range(N_ATT_BLK):
        cols = att[:, blk * ATT_W:(blk + 1) * ATT_W]
        att_blocks.append(cols if blk % 3 == 2 else _att_head_order(cols))
    w = jnp.concatenate([gate, rv, rg, sg, rqk] + att_blocks, axis=1)
    return w.astype(BF16), _rot_partner_cols(rqk, RET_QK_DIM, RET_QK_DIM).astype(BF16)


def _in_proj(x2d, gain, w_bf16, wrot_bf16, tabs, S, tm):
    T = x2d.shape[0]
    nrep = S // tm
    tab_spec = pl.BlockSpec((tm, LANES), lambda i, j: (i % nrep, 0))

    return pl.pallas_call(
        _in_proj_kernel,
        grid=(T // tm, N_COL_BLK),
        in_specs=[
            pl.BlockSpec((tm, D_MODEL), lambda i, j: (i, 0)),
            pl.BlockSpec((1, D_MODEL), lambda i, j: (0, 0)),
            pl.BlockSpec((D_MODEL, COL_BLK), lambda i, j: (0, j)),
            pl.BlockSpec((D_MODEL, COL_BLK), lambda i, j: (0, 0)),
            tab_spec, tab_spec, tab_spec, tab_spec,
        ],
        out_specs=[
            pl.BlockSpec((tm, COL_BLK), lambda i, j: (i, jnp.minimum(j, N_PROJ_BLK - 1))),
            pl.BlockSpec((tm, ATTP_BLK), lambda i, j: (i, jnp.maximum(j - CB_ATT, 0))),
        ],
        out_shape=[
            jax.ShapeDtypeStruct((T, N_PROJ_BLK * COL_BLK), BF16),
            jax.ShapeDtypeStruct((T, N_ATT_BLK * ATTP_BLK), U32),
        ],
        scratch_shapes=[pltpu.VMEM((tm, D_MODEL), BF16)],
        compiler_params=_cparams(("arbitrary", "arbitrary")),
        name="in_proj",
    )(x2d, gain.reshape(1, D_MODEL), w_bf16, wrot_bf16, *tabs)


RET_TILE = 512
RET_CPT = RET_TILE // RET_CHUNK


def _retention_kernel(lg_ref, q_ref, k_ref, v_ref, g_ref, gn_ref, o_ref,
                      dmat_ref, tab_ref, sf_ref, sbrun_ref, sb_ref, *, n_tiles):
    b, ph, i = pl.program_id(0), pl.program_id(1), pl.program_id(2)
    C, H, DK, DV = RET_CHUNK, RET_HEADS, RET_QK_DIM, RET_V_DIM

    @pl.when(jnp.logical_and(b == 0, jnp.logical_and(ph == 0, i == 0)))
    def _():
        row = lax.broadcasted_iota(I32, (C, C), 0).astype(F32)
        col = lax.broadcasted_iota(I32, (C, C), 1).astype(F32)
        diff = row - col
        for h in range(H):
            fwd = jnp.exp(jnp.maximum(diff, 0.0) * lg_ref[0, h])
            bwd = jnp.exp(jnp.maximum(-diff, 0.0) * lg_ref[1, h])
            dmat_ref[h] = jnp.where(diff >= 0, fwd, bwd)
        pos = lax.broadcasted_iota(I32, (C, RET_QK_W), 0).astype(F32)
        head = lax.broadcasted_iota(I32, (C, RET_QK_W), 1) >> 6
        lgf = jnp.zeros((C, RET_QK_W), F32)
        lgb = jnp.zeros((C, RET_QK_W), F32)
        for h in range(H):
            lgf = jnp.where(head == h, lg_ref[0, h], lgf)
            lgb = jnp.where(head == h, lg_ref[1, h], lgb)
        tab_ref[0] = jnp.exp((C - 1.0 - pos) * lgf)
        tab_ref[1] = jnp.exp((pos + 1.0) * lgf)
        tab_ref[2] = jnp.exp(pos * lgb)
        tab_ref[3] = jnp.exp((C - pos) * lgb)
        tab_ref[4] = jnp.exp(C * lgf)
        tab_ref[5] = jnp.exp(C * lgb)

    def chunk_decay(idx):
        return tab_ref[idx][:DV, :].T

    def kv_state(kd, v):
        kdt = kd.T.astype(BF16)
        parts = [jnp.dot(kdt[h * DK:(h + 1) * DK, :], v[:, h * DV:(h + 1) * DV], preferred_element_type=F32)
                 for h in range(H)]
        return jnp.concatenate(parts, axis=0)

    @pl.when(ph == 0)
    def _():
        @pl.when(i == 0)
        def _():
            sbrun_ref[...] = jnp.zeros_like(sbrun_ref)

        tile = n_tiles - 1 - i
        dec = chunk_decay(5)
        for c in reversed(range(RET_CPT)):
            rows = pl.ds(c * C, C)
            sb_ref[tile * RET_CPT + c] = sbrun_ref[...]
            kd = k_ref[rows, :].astype(F32) * tab_ref[2]
            sbrun_ref[...] = dec * sbrun_ref[...] + kv_state(kd, v_ref[rows, :])

    @pl.when(ph == 1)
    def _():
        @pl.when(i == 0)
        def _():
            sf_ref[...] = jnp.zeros_like(sf_ref)

        dec = chunk_decay(4)
        head = lax.broadcasted_iota(I32, (C, RET_QK_W), 1) >> 6
        for c in range(RET_CPT):
            rows = pl.ds(c * C, C)
            q = q_ref[rows, :].astype(F32)
            k = k_ref[rows, :].astype(F32)
            v = v_ref[rows, :]
            kt = k.T.astype(BF16)
            qf = q * tab_ref[1]
            qb = q * tab_ref[3]
            sf = sf_ref[...].astype(BF16)
            sb = sb_ref[i * RET_CPT + c].astype(BF16)
            outs = []
            for h in range(H):
                sel = head == h
                qh = jnp.where(sel, q, 0.0).astype(BF16)
                s = jnp.dot(qh, kt, preferred_element_type=F32) * dmat_ref[h]
                o = jnp.dot(s.astype(BF16), v[:, h * DV:(h + 1) * DV], preferred_element_type=F32)
                o += jnp.dot(jnp.where(sel, qf, 0.0).astype(BF16), sf, preferred_element_type=F32)
                o += jnp.dot(jnp.where(sel, qb, 0.0).astype(BF16), sb, preferred_element_type=F32)
                mu = jnp.mean(o, axis=-1, keepdims=True)
                var = jnp.mean(jnp.square(o - mu), axis=-1, keepdims=True)
                outs.append((o - mu) * lax.rsqrt(var + NORM_EPS))
            y = jnp.concatenate(outs, axis=1) * gn_ref[...]
            o_ref[rows, :] = (y * g_ref[rows, :].astype(F32)).astype(BF16)
            sf_ref[...] = dec * sf_ref[...] + kv_state(k * tab_ref[0], v)


def _retention(proj, lg, gn_gain, B, S):
    nt = S // RET_TILE
    nc = S // RET_CHUNK
    qkw = RET_QK_W

    def kv_idx(blk):
        def f(b, ph, i, lg_ref):
            t = jnp.where(ph == 0, nt - 1 - i, i)
            return (b * nt + t, blk)
        return f

    def out_idx(blk):
        def f(b, ph, i, lg_ref):
            return (b * nt + i * ph, blk)
        return f

    grid_spec = pltpu.PrefetchScalarGridSpec(
        num_scalar_prefetch=1,
        grid=(B, 2, nt),
        in_specs=[
            pl.BlockSpec((RET_TILE, qkw), out_idx(2 * CB_RQK)),
            pl.BlockSpec((RET_TILE, qkw), kv_idx(2 * CB_RQK + 1)),
            pl.BlockSpec((RET_TILE, RET_V_W), kv_idx(CB_RV)),
            pl.BlockSpec((RET_TILE, RET_V_W), out_idx(CB_RG)),
            pl.BlockSpec((1, RET_V_W), lambda b, ph, i, lg_ref: (0, 0)),
        ],
        out_specs=pl.BlockSpec((RET_TILE, RET_V_W), out_idx(0)),
        scratch_shapes=[
            pltpu.VMEM((RET_HEADS, RET_CHUNK, RET_CHUNK), F32),
            pltpu.VMEM((6, RET_CHUNK, qkw), F32),
            pltpu.VMEM((qkw, RET_V_DIM), F32),
            pltpu.VMEM((qkw, RET_V_DIM), F32),
            pltpu.VMEM((nc, qkw, RET_V_DIM), F32),
        ],
    )
    return pl.pallas_call(
        functools.partial(_retention_kernel, n_tiles=nt),
        grid_spec=grid_spec,
        out_shape=jax.ShapeDtypeStruct((B * S, RET_V_W), BF16),
        compiler_params=_cparams(("arbitrary", "arbitrary", "arbitrary")),
        name="retention",
    )(lg, proj, proj, proj, proj, gn_gain.reshape(1, RET_V_W))


ATT_QB = 128
LOG2E = 1.4426950408889634


def _dil_attn_kernel(q_ref, km_ref, kp_ref, kn_ref, vm_ref, vp_ref, vn_ref, o_ref, lse_ref, *, L, Lq, d):
    i = pl.program_id(1)
    R, QB, HD = ATT_R, ATT_QB, ATT_HEAD_DIM
    nq = Lq // QB
    NK = QB + 2 * R
    row = lax.broadcasted_iota(I32, (QB, NK), 0)
    col = lax.broadcasted_iota(I32, (QB, NK), 1)
    band = jnp.logical_and(col - row >= 0, col - row <= 2 * R)
    lane = lax.broadcasted_iota(I32, (QB, LANES), 1)
    scale = HD ** -0.5

    def rows(ref, r, n):
        return ref[pl.ds(r, n, stride=d), :] if d > 1 else ref[...]

    def residue(r):
        q_all = rows(q_ref, r, Lq)
        k_all = jnp.concatenate([rows(kp_ref, r, R), rows(km_ref, r, Lq), rows(kn_ref, r, R)], axis=0)
        v_all = jnp.concatenate([rows(vp_ref, r, R), rows(vm_ref, r, Lq), rows(vn_ref, r, R)], axis=0)
        for qb in range(nq):
            base = i * Lq + qb * QB - R
            ok = jnp.logical_and(band, jnp.logical_and(col >= -base, col < L - base))
            q2 = _unpack_halves(q_all[qb * QB:(qb + 1) * QB])
            k2 = _unpack_halves(k_all[qb * QB:qb * QB + NK])
            v2 = _unpack_halves(v_all[qb * QB:qb * QB + NK])
            outs, lses = [], []
            for half in range(2):
                s = lax.dot_general(q2[half].astype(BF16), k2[half].astype(BF16), (((1,), (1,)), ((), ())),
                                    preferred_element_type=F32)
                s = jnp.where(ok, s, NEG_BIG)
                m = jnp.max(s, axis=-1, keepdims=True)
                p = jnp.exp2((s - m) * (scale * LOG2E))
                den = jnp.sum(p, axis=-1, keepdims=True)
                pn = (p * (1.0 / den)).astype(BF16)
                outs.append(jnp.dot(pn, v2[half].astype(BF16), preferred_element_type=F32))
                lses.append(m * scale + jnp.log(den))
            packed = _pack_halves(jnp.concatenate(outs, axis=1))
            lse_tile = jnp.where(lane < LANES // 2, lses[0], lses[1])
            dst = pl.ds(r + qb * QB * d, QB, stride=d) if d > 1 else pl.ds(qb * QB, QB)
            o_ref[dst, :] = packed
            lse_ref[dst, :] = lse_tile

    if d > 1:
        def body(r, carry):
            residue(r)
            return carry

        lax.fori_loop(0, d, body, 0)
    else:
        residue(0)


def _dil_attn(attp, g, B, S):
    window, d = ATT_PATTERNS[g]
    L = S // d
    Lq = min(L, 512 if d == 1 else 256)
    tb = Lq * d
    nb = S // tb
    halo = ATT_R * d
    hb = tb // halo
    nhb = S // halo
    npair = ATTP_BLK // LANES
    cq, ck, cv = (npair * (3 * g + kind) for kind in range(3))

    def main(c):
        return pl.BlockSpec((tb, LANES), lambda b, i, p: (b * nb + i, c + p))

    def prev(c):
        return pl.BlockSpec((halo, LANES), lambda b, i, p: (b * nhb + jnp.maximum(i * hb - 1, 0), c + p))

    def nxt(c):
        return pl.BlockSpec((halo, LANES), lambda b, i, p: (b * nhb + jnp.minimum((i + 1) * hb, nhb - 1), c + p))

    out_spec = pl.BlockSpec((tb, LANES), lambda b, i, p: (b * nb + i, p))
    return pl.pallas_call(
        functools.partial(_dil_attn_kernel, L=L, Lq=Lq, d=d),
        grid=(B, nb, npair),
        in_specs=[main(cq), main(ck), prev(ck), nxt(ck), main(cv), prev(cv), nxt(cv)],
        out_specs=[out_spec, out_spec],
        out_shape=[
            jax.ShapeDtypeStruct((B * S, npair * LANES), U32),
            jax.ShapeDtypeStruct((B * S, npair * LANES), F32),
        ],
        compiler_params=_cparams(("arbitrary", "arbitrary", "arbitrary")),
        name=f"dil_attn_{g}",
    )(attp, attp, attp, attp, attp, attp, attp)


MERGE_TILE = 512


def _merge_kernel(x_ref, r_ref, o0_ref, o1_ref, o2_ref, l0_ref, l1_ref, l2_ref, su_ref, sv_ref,
                  g0_ref, g1_ref, g2_ref, wr_ref, wa_ref, ws_ref, wo_ref, sgw_ref, sgb_ref, lng_ref,
                  nf_ref, wrt_ref, brt_ref, tri_ref,
                  x2_ref, hp_ref, route_ref, cnt_ref, carry_ref):
    i = pl.program_id(0)
    tm = x_ref.shape[0]
    HD = ATT_HEAD_DIM

    @pl.when(i == 0)
    def _():
        carry_ref[...] = jnp.zeros_like(carry_ref)

    l0, l1, l2 = l0_ref[...], l1_ref[...], l2_ref[...]
    m = jnp.maximum(jnp.maximum(l0, l1), l2)
    e0, e1, e2 = jnp.exp(l0 - m), jnp.exp(l1 - m), jnp.exp(l2 - m)
    inv = 1.0 / (e0 + e1 + e2)
    wts = (e0 * inv, e1 * inv, e2 * inv)
    outs = [jnp.concatenate(_unpack_halves(o_ref[...]), axis=1) for o_ref in (o0_ref, o1_ref, o2_ref)]
    heads = []
    for h in range(ATT_HEADS):
        cols = slice(h * HD, (h + 1) * HD)
        c = (h % 2) * LANES + (h // 2) * (LANES // 2)
        heads.append(sum(w[:, c:c + 1] * o[:, cols] for w, o in zip(wts, outs)))
    att = jnp.concatenate(heads, axis=1).astype(BF16)
    y_att = jnp.dot(att, wa_ref[...], preferred_element_type=F32)
    y_ret = jnp.dot(r_ref[...], wr_ref[...], preferred_element_type=F32)

    u = su_ref[...].astype(F32)
    v = sv_ref[...].astype(F32)
    mu = jnp.mean(v, axis=-1, keepdims=True)
    var = jnp.mean(jnp.square(v - mu), axis=-1, keepdims=True)
    vn = ((v - mu) * lax.rsqrt(var + NORM_EPS) * lng_ref[...]).astype(BF16)
    gw = SG_WIDTH // SG_GROUPS
    chunks = []
    for c in range(tm // SG_CHUNK):
        rows = slice(c * SG_CHUNK, (c + 1) * SG_CHUNK)
        parts = [jnp.dot(sgw_ref[g], vn[rows, g * gw:(g + 1) * gw], preferred_element_type=F32)
                 for g in range(SG_GROUPS)]
        chunks.append(jnp.concatenate(parts, axis=1) + sgb_ref[...])
    vs = jnp.concatenate(chunks, axis=0)
    y_sg = jnp.dot((u * vs).astype(BF16), ws_ref[...], preferred_element_type=F32)

    merged = (g0_ref[...].astype(F32) * y_ret + g1_ref[...].astype(F32) * y_att
              + g2_ref[...].astype(F32) * y_sg)
    x2 = x_ref[...] + jnp.dot(merged.astype(BF16), wo_ref[...], preferred_element_type=F32)
    x2_ref[...] = x2

    ms = jnp.mean(x2 * x2, axis=-1, keepdims=True)
    h2 = x2 * lax.rsqrt(ms + NORM_EPS) * nf_ref[...]
    hp_ref[...] = _pack_halves(h2)

    h_hi = h2.astype(BF16)
    h_lo = (h2 - h_hi.astype(F32)).astype(BF16)
    hw = jnp.dot(h_hi, wrt_ref[...], preferred_element_type=F32)
    logits = (hw[:, :LANES] + hw[:, LANES:] + jnp.dot(h_lo, wrt_ref[:, :LANES], preferred_element_type=F32)
              + brt_ref[...])
    lane = lax.broadcasted_iota(I32, logits.shape, 1)
    big = jnp.int32(LANES)
    is_g = lane < N_GROUPS
    gl = jnp.where(is_g, logits, -jnp.inf)
    gmax = jnp.max(gl, axis=-1, keepdims=True)
    g_idx = jnp.min(jnp.where(jnp.logical_and(is_g, gl == gmax), lane, big), axis=-1, keepdims=True)
    g_w = 1.0 / jnp.sum(jnp.where(is_g, jnp.exp(gl - gmax), 0.0), axis=-1, keepdims=True)
    lo = ROUTE_LANE0 + g_idx * EXP_PER_GROUP
    in_grp = jnp.logical_and(lane >= lo, lane < lo + EXP_PER_GROUP)
    el = jnp.where(in_grp, logits, -jnp.inf)
    v1 = jnp.max(el, axis=-1, keepdims=True)
    i1 = jnp.min(jnp.where(el == v1, lane, big), axis=-1, keepdims=True)
    el2 = jnp.where(lane == i1, -jnp.inf, el)
    v2 = jnp.max(el2, axis=-1, keepdims=True)
    i2 = jnp.min(jnp.where(el2 == v2, lane, big), axis=-1, keepdims=True)
    t = jnp.exp(v2 - v1)
    w1 = g_w / (1.0 + t)
    w2 = g_w * t / (1.0 + t)

    oh1 = lane == i1
    oh2 = lane == i2
    both = jnp.where(jnp.logical_or(oh1, oh2), 1.0, 0.0)
    before = jnp.dot(tri_ref[...], both.astype(BF16), preferred_element_type=F32) + carry_ref[0:1, :]
    rank1 = jnp.sum(jnp.where(oh1, before, 0.0), axis=-1, keepdims=True)
    rank2 = jnp.sum(jnp.where(oh2, before, 0.0), axis=-1, keepdims=True)
    carry = carry_ref[0:1, :] + jnp.sum(both, axis=0, keepdims=True)
    carry_ref[...] = jnp.broadcast_to(carry, carry_ref.shape)
    cnt_ref[...] = jnp.broadcast_to(carry, cnt_ref.shape)

    out = jnp.where(lane == 0, (i1 - ROUTE_LANE0).astype(F32), 0.0)
    out = jnp.where(lane == 1, (i2 - ROUTE_LANE0).astype(F32), out)
    out = jnp.where(lane == 2, w1, out)
    out = jnp.where(lane == 3, w2, out)
    out = jnp.where(lane == 4, rank1, out)
    out = jnp.where(lane == 5, rank2, out)
    route_ref[...] = out


def _merge(x2d, r, os_, lses, proj, lw):
    T = x2d.shape[0]
    tm = MERGE_TILE
    row = lambda w: pl.BlockSpec((tm, w), lambda i: (i, 0))
    pcol = lambda w, cb: pl.BlockSpec((tm, w), lambda i: (i, cb))
    full = lambda a: pl.BlockSpec(a.shape, lambda i: (0,) * a.ndim)
    gate_cb = CB_GATE * COL_BLK // D_MODEL
    weights = [lw["w_ret_out"], lw["w_att_out"], lw["w_sg_out"], lw["w_o"], lw["sg_w"], lw["sg_b_full"],
               lw["sg_ln_gain"], lw["norm_ffn"], lw["w_router"], lw["b_router"], lw["tri"]]
    return pl.pallas_call(
        _merge_kernel,
        grid=(T // tm,),
        in_specs=[row(D_MODEL), row(RET_V_W)] + [row(2 * LANES)] * (2 * len(ATT_PATTERNS)) + [
                  pcol(SG_WIDTH, CB_SGU), pcol(SG_WIDTH, CB_SGV),
                  pcol(D_MODEL, gate_cb), pcol(D_MODEL, gate_cb + 1), pcol(D_MODEL, gate_cb + 2)
                 ] + [full(a) for a in weights],
        out_specs=[row(D_MODEL), row(D_MODEL // 2), row(LANES), pl.BlockSpec((8, LANES), lambda i: (0, 0))],
        out_shape=[jax.ShapeDtypeStruct((T, D_MODEL), F32), jax.ShapeDtypeStruct((T, D_MODEL // 2), U32),
                   jax.ShapeDtypeStruct((T, LANES), F32), jax.ShapeDtypeStruct((8, LANES), F32)],
        scratch_shapes=[pltpu.VMEM((8, LANES), F32)],
        compiler_params=_cparams(("arbitrary",)),
        name="merge",
    )(x2d, r, *os_, *lses, proj, proj, proj, proj, proj, *weights)


DISPATCH_TILE = 1024
ISSUE_UNROLL = 8


def _dispatch_kernel(dest_ref, h_ref, xb_in_ref, xb_ref, sem):
    del xb_in_ref
    tm = DISPATCH_TILE

    def issue(t, carry):
        src = h_ref.at[pl.ds(t, 1)]
        for k in range(TOP_K):
            pltpu.make_async_copy(src, xb_ref.at[pl.ds(dest_ref[0, 0, TOP_K * t + k], 1)], sem).start(priority=k)
        return carry

    lax.fori_loop(0, tm, issue, 0, unroll=ISSUE_UNROLL)
    pltpu.make_async_copy(xb_ref.at[pl.ds(0, TOP_K * tm)], xb_ref.at[pl.ds(0, TOP_K * tm)], sem).wait()


def _dispatch(hp, dest, P):
    T = hp.shape[0]
    tm = DISPATCH_TILE
    half = D_MODEL // 2
    return pl.pallas_call(
        _dispatch_kernel,
        grid=(T // tm,),
        in_specs=[
            pl.BlockSpec((1, 1, TOP_K * tm), lambda i: (i, 0, 0), memory_space=pltpu.SMEM),
            pl.BlockSpec((tm, half), lambda i: (i, 0)),
            pl.BlockSpec(memory_space=pl.ANY),
        ],
        out_specs=pl.BlockSpec(memory_space=pl.ANY),
        out_shape=jax.ShapeDtypeStruct((P, half), U32),
        scratch_shapes=[pltpu.SemaphoreType.DMA(())],
        input_output_aliases={2: 0},
        compiler_params=_cparams(("arbitrary",)),
        name="dispatch",
    )(dest.reshape(T // tm, 1, TOP_K * tm), hp, jnp.zeros((P, half), U32))


def _experts_kernel(blk_e_ref, nused_ref, x_ref, wg_ref, wu_ref, wd_ref, y_ref):
    i = pl.program_id(0)

    @pl.when(i < nused_ref[0])
    def _():
        x = jnp.concatenate(_unpack_halves(x_ref[...]), axis=1).astype(BF16)
        gate = jnp.dot(x, wg_ref[0], preferred_element_type=F32)
        up = jnp.dot(x, wu_ref[0], preferred_element_type=F32)
        hid = (gate * _sigmoid(gate) * up).astype(BF16)
        y_ref[...] = jnp.dot(hid, wd_ref[0], preferred_element_type=F32)

    @pl.when(i >= nused_ref[0])
    def _():
        y_ref[...] = jnp.zeros_like(y_ref)


def _experts(xb, blk_e, nused, wg, wu, wd):
    P = xb.shape[0]
    bm = MOE_BLOCK
    grid_spec = pltpu.PrefetchScalarGridSpec(
        num_scalar_prefetch=2,
        grid=(P // bm,),
        in_specs=[
            pl.BlockSpec((bm, D_MODEL // 2), lambda i, e, n: (i, 0)),
            pl.BlockSpec((1, D_MODEL, EXP_HIDDEN), lambda i, e, n: (e[i], 0, 0)),
            pl.BlockSpec((1, D_MODEL, EXP_HIDDEN), lambda i, e, n: (e[i], 0, 0)),
            pl.BlockSpec((1, EXP_HIDDEN, D_MODEL), lambda i, e, n: (e[i], 0, 0)),
        ],
        out_specs=pl.BlockSpec((bm, D_MODEL), lambda i, e, n: (i, 0)),
    )
    return pl.pallas_call(
        _experts_kernel,
        grid_spec=grid_spec,
        out_shape=jax.ShapeDtypeStruct((P, D_MODEL), F32),
        compiler_params=_cparams(("arbitrary",)),
        name="experts",
    )(blk_e, nused, xb, wg, wu, wd)


COMBINE_TILE = 256


def _combine_kernel(dest_ref, dest_next_ref, x_ref, route_ref, gain_ref, y_ref, o_ref, buf_ref, sems, *, final_norm):
    i = pl.program_id(0)
    n = pl.num_programs(0)
    tm = COMBINE_TILE

    def gather(idx_ref, slot):
        def issue(t, carry):
            for k in range(TOP_K):
                pltpu.make_async_copy(y_ref.at[pl.ds(idx_ref[0, 0, TOP_K * t + k], 1)],
                                      buf_ref.at[slot, k, pl.ds(t, 1)], sems.at[slot]).start(priority=k)
            return carry

        lax.fori_loop(0, tm, issue, 0, unroll=ISSUE_UNROLL)

    slot = i % 2

    @pl.when(i == 0)
    def _():
        gather(dest_ref, 0)

    @pl.when(i + 1 < n)
    def _():
        gather(dest_next_ref, 1 - slot)

    pltpu.make_async_copy(buf_ref.at[slot], buf_ref.at[slot], sems.at[slot]).wait()
    route = route_ref[...]
    x = x_ref[...] + route[:, 2:3] * buf_ref[slot, 0] + route[:, 3:4] * buf_ref[slot, 1]
    if final_norm:
        ms = jnp.mean(x * x, axis=-1, keepdims=True)
        x = x * lax.rsqrt(ms + NORM_EPS) * gain_ref[...]
    o_ref[...] = x


def _combine(x2, route, yb, dest, gain, final_norm):
    T = x2.shape[0]
    tm = COMBINE_TILE
    nt = T // tm
    dest3 = dest.reshape(nt, 1, TOP_K * tm)
    return pl.pallas_call(
        functools.partial(_combine_kernel, final_norm=final_norm),
        grid=(nt,),
        in_specs=[
            pl.BlockSpec((1, 1, TOP_K * tm), lambda i: (i, 0, 0), memory_space=pltpu.SMEM),
            pl.BlockSpec((1, 1, TOP_K * tm), lambda i: (jnp.minimum(i + 1, nt - 1), 0, 0), memory_space=pltpu.SMEM),
            pl.BlockSpec((tm, D_MODEL), lambda i: (i, 0)),
            pl.BlockSpec((tm, LANES), lambda i: (i, 0)),
            pl.BlockSpec((1, D_MODEL), lambda i: (0, 0)),
            pl.BlockSpec(memory_space=pl.ANY),
        ],
        out_specs=pl.BlockSpec((tm, D_MODEL), lambda i: (i, 0)),
        out_shape=jax.ShapeDtypeStruct((T, D_MODEL), F32),
        scratch_shapes=[pltpu.VMEM((2, TOP_K, tm, D_MODEL), F32), pltpu.SemaphoreType.DMA((2,))],
        compiler_params=_cparams(("arbitrary",)),
        name="combine_final" if final_norm else "combine",
    )(dest3, dest3, x2, route, gain.reshape(1, D_MODEL), yb)


def _layer_weights(l, p):
    w_router = jnp.zeros((D_MODEL, LANES), F32)
    w_router = w_router.at[:, :N_GROUPS].set(p["w_router_group"][l])
    w_router = w_router.at[:, ROUTE_LANE0:ROUTE_LANE0 + N_EXPERTS].set(p["w_router_expert"][l])
    b_router = jnp.zeros((1, LANES), F32)
    b_router = b_router.at[0, :N_GROUPS].set(p["b_router_group"][l])
    b_router = b_router.at[0, ROUTE_LANE0:ROUTE_LANE0 + N_EXPERTS].set(p["b_router_expert"][l])
    tm = MERGE_TILE
    tri = (jnp.arange(tm)[:, None] > jnp.arange(tm)[None, :]).astype(BF16)
    gw = SG_WIDTH // SG_GROUPS
    w_in, w_in_rot = _in_proj_weights(p["w_in"][l])
    return {
        "norm_mix": p["norm_mix"][l],
        "w_in": w_in,
        "w_in_rot": w_in_rot,
        "lg": jnp.stack([-jnp.exp(p["ret_decay_fwd"][l].astype(F32)), -jnp.exp(p["ret_decay_bwd"][l].astype(F32))]),
        "ret_gn_gain": p["ret_gn_gain"][l],
        "w_ret_out": p["w_ret_out"][l].astype(BF16),
        "w_att_out": p["w_att_out"][l].astype(BF16),
        "w_sg_out": p["w_sg_out"][l].astype(BF16),
        "w_o": p["w_o"][l].astype(BF16),
        "sg_w": p["sg_w"][l].astype(BF16),
        "sg_b_full": jnp.repeat(p["sg_b"][l].T, gw, axis=1),
        "sg_ln_gain": p["sg_ln_gain"][l].reshape(1, SG_WIDTH),
        "norm_ffn": p["norm_ffn"][l].reshape(1, D_MODEL),
        "w_router": jnp.concatenate([w_router.astype(BF16),
                                     (w_router - w_router.astype(BF16).astype(F32)).astype(BF16)], axis=1),
        "b_router": b_router,
        "tri": tri,
        "w_exp_gate": p["w_exp_gate"][l].astype(BF16),
        "w_exp_up": p["w_exp_up"][l].astype(BF16),
        "w_exp_down": p["w_exp_down"][l].astype(BF16),
    }


def _moe_plan(route, cnt, T):
    bm = MOE_BLOCK
    eid = route[:, 0:TOP_K].astype(I32)
    rank = route[:, 4:4 + TOP_K].astype(I32)
    counts = cnt[0, ROUTE_LANE0:ROUTE_LANE0 + N_EXPERTS].astype(I32)
    padded = (counts + bm - 1) // bm * bm
    pad_end = jnp.cumsum(padded)
    pad_start = pad_end - padded
    experts = jnp.arange(N_EXPERTS, dtype=I32)
    dest = jnp.sum(jnp.where(eid[..., None] == experts, pad_start, 0), axis=-1) + rank
    P = T * TOP_K + N_EXPERTS * bm
    nblk = P // bm
    blk_start = jnp.arange(nblk, dtype=I32) * bm
    blk_e = jnp.minimum(jnp.sum((pad_end[None, :] <= blk_start[:, None]).astype(I32), axis=1), N_EXPERTS - 1)
    nused = (pad_end[-1:] // bm).astype(I32)
    return dest.reshape(-1), blk_e, nused, P


def _run_trunk(x, lws, tabs, norm_final):
    B, S, _ = x.shape
    T = B * S
    x2d = x.reshape(T, D_MODEL)
    tm_in = min(2048, S)
    for l, lw in enumerate(lws):
        proj, attp = _in_proj(x2d, lw["norm_mix"], lw["w_in"], lw["w_in_rot"], tabs, S, tm_in)
        r = _retention(proj, lw["lg"], lw["ret_gn_gain"], B, S)
        os_, lses = zip(*[_dil_attn(attp, g, B, S) for g in range(len(ATT_PATTERNS))])
        x2, hp, route, cnt = _merge(x2d, r, os_, lses, proj, lw)
        dest, blk_e, nused, P = _moe_plan(route, cnt, T)
        xb = _dispatch(hp, dest, P)
        yb = _experts(xb, blk_e, nused, lw["w_exp_gate"], lw["w_exp_up"], lw["w_exp_down"])
        x2d = _combine(x2, route, yb, dest, norm_final, final_norm=(l == len(lws) - 1))
    return x2d.reshape(B, S, D_MODEL)


def kernel(x_prompt, x_sample, norm_mix, w_in, ret_decay_fwd, ret_decay_bwd, ret_gn_gain, sg_ln_gain, sg_w, sg_b,
           w_ret_out, w_att_out, w_sg_out, w_o, norm_ffn, w_router_group, b_router_group, w_router_expert,
           b_router_expert, w_exp_gate, w_exp_up, w_exp_down, norm_final):
    p = dict(norm_mix=norm_mix, w_in=w_in, ret_decay_fwd=ret_decay_fwd, ret_decay_bwd=ret_decay_bwd,
             ret_gn_gain=ret_gn_gain, sg_ln_gain=sg_ln_gain, sg_w=sg_w, sg_b=sg_b, w_ret_out=w_ret_out,
             w_att_out=w_att_out, w_sg_out=w_sg_out, w_o=w_o, norm_ffn=norm_ffn, w_router_group=w_router_group,
             b_router_group=b_router_group, w_router_expert=w_router_expert, b_router_expert=b_router_expert,
             w_exp_gate=w_exp_gate, w_exp_up=w_exp_up, w_exp_down=w_exp_down)
    lws = [_layer_weights(l, p) for l in range(DEPTH)]
    s_max = max(x_prompt.shape[1], x_sample.shape[1])
    tabs = _ret_rot_tables(s_max) + _att_rot_tables(s_max)
    y_prompt = _run_trunk(x_prompt, lws, tabs, norm_final)
    y_sample = _run_trunk(x_sample, lws, tabs, norm_final)
    return (y_prompt, y_sample)
```

```python
import functools

import jax
import jax.numpy as jnp
import numpy as np
from jax import lax
from jax.experimental import pallas as pl
from jax.experimental.pallas import tpu as pltpu

F32 = jnp.float32
BF16 = jnp.bfloat16
U32 = jnp.uint32
I32 = jnp.int32

D_MODEL = 1024
DEPTH = 2
NORM_EPS = 1e-6
RET_HEADS = 4
RET_QK_DIM = 64
RET_V_DIM = 128
RET_CHUNK = 128
RET_THETA = 10000.0
ATT_PATTERNS = ((128, 1), (512, 4), (2048, 16))
ATT_HEADS = 4
ATT_HEAD_DIM = 128
ATT_ROT_DIM = ATT_HEAD_DIM // 4
ROPE_THETA = 500000.0
SG_GROUPS = 4
SG_CHUNK = 128
SG_WIDTH = 512
N_BRANCH = 3
N_GROUPS = 4
EXP_PER_GROUP = 8
N_EXPERTS = N_GROUPS * EXP_PER_GROUP
TOP_K = 2
EXP_HIDDEN = 512
RET_QK_W = RET_HEADS * RET_QK_DIM
RET_V_W = RET_HEADS * RET_V_DIM
ATT_W = ATT_HEADS * ATT_HEAD_DIM
ATT_IN_W = len(ATT_PATTERNS) * 3 * ATT_W
SG_IN_W = 2 * SG_WIDTH
GATE_IN_W = N_BRANCH * D_MODEL
N_IN = 2 * RET_QK_W + 2 * RET_V_W + ATT_IN_W + SG_IN_W + GATE_IN_W
NEG_BIG = -1e30

LANES = 128
COL_BLK = 512
N_COL_BLK = N_IN // COL_BLK
ATT_R = 64
assert all(w // (2 * d) == ATT_R for w, d in ATT_PATTERNS)
CB_GATE, CB_RV, CB_RG, CB_SGU, CB_SGV, CB_RQK, CB_ATT = 0, 6, 7, 8, 9, 10, 11
N_PROJ_BLK = CB_ATT
N_ATT_BLK = N_COL_BLK - CB_ATT
ATT_PAIR_SHIFT = LANES // 2
ATTP_BLK = COL_BLK // 2
ROUTE_LANE0 = N_GROUPS
MOE_BLOCK = 512
VMEM_LIMIT = 56 * 1024 * 1024
HI_HALF = 0xFFFF0000


def _cparams(sem, vmem=VMEM_LIMIT):
    return pltpu.CompilerParams(dimension_semantics=sem, vmem_limit_bytes=vmem)


def _sigmoid(x):
    return 0.5 * jnp.tanh(0.5 * x) + 0.5


def _pack_halves(y):
    n = y.shape[1] // 2
    bits = lax.bitcast_convert_type(y.astype(BF16).astype(F32), U32)
    return (bits[:, :n] >> 16) | (bits[:, n:] & jnp.uint32(HI_HALF))


def _unpack_halves(p):
    return (lax.bitcast_convert_type(p << 16, F32), lax.bitcast_convert_type(p & jnp.uint32(HI_HALF), F32))


def _in_proj_kernel(x_ref, g_ref, w_ref, wrot_ref, rc_ref, rs_ref, ac_ref, as_ref, o_ref, oa_ref, h_ref):
    j = pl.program_id(1)

    @pl.when(j == 0)
    def _():
        x = x_ref[...]
        ms = jnp.mean(x * x, axis=-1, keepdims=True)
        h_ref[...] = (x * lax.rsqrt(ms + NORM_EPS) * g_ref[...]).astype(BF16)

    def dot(w):
        return jnp.dot(h_ref[...], w[...], preferred_element_type=F32)

    reps = COL_BLK // LANES

    def tiled(tab_ref):
        return jnp.concatenate([tab_ref[...]] * reps, axis=1)

    att_kind = (j - CB_ATT) % 3

    @pl.when(j < CB_RV)
    def _():
        o_ref[...] = _sigmoid(dot(w_ref)).astype(BF16)

    @pl.when(j == CB_RV)
    def _():
        o_ref[...] = dot(w_ref).astype(BF16)

    @pl.when(j == CB_RG)
    def _():
        acc = dot(w_ref)
        o_ref[...] = (acc * _sigmoid(acc)).astype(BF16)

    @pl.when(jnp.logical_or(j == CB_SGU, j == CB_SGV))
    def _():
        o_ref[...] = jax.nn.gelu(dot(w_ref)).astype(BF16)

    @pl.when(j == CB_RQK)
    def _():
        y = dot(w_ref) * tiled(rc_ref) + dot(wrot_ref) * tiled(rs_ref)
        lane = lax.broadcasted_iota(I32, y.shape, 1)
        y = jnp.where(lane >= RET_QK_W, y * (RET_QK_DIM ** -0.5), y)
        o_ref[...] = y.astype(BF16)

    @pl.when(jnp.logical_and(j >= CB_ATT, att_kind != 2))
    def _():
        acc = dot(w_ref)
        partner = jnp.concatenate([pltpu.roll(acc[:, c * LANES:(c + 1) * LANES], ATT_PAIR_SHIFT, 1)
                                   for c in range(reps)], axis=1)
        oa_ref[...] = _pack_halves(acc * tiled(ac_ref) + partner * tiled(as_ref))

    @pl.when(jnp.logical_and(j >= CB_ATT, att_kind == 2))
    def _():
        oa_ref[...] = _pack_halves(dot(w_ref))


def _rot_angles(S, n_rot, theta):
    half = n_rot // 2
    inv = jnp.power(jnp.float32(theta), -jnp.arange(half, dtype=F32) * (2.0 / n_rot))
    ang = jnp.arange(S, dtype=F32)[:, None] * inv[None, :]
    return jnp.cos(ang), jnp.sin(ang)


def _ret_rot_tables(S):
    cos, sin = _rot_angles(S, RET_QK_DIM, RET_THETA)
    return jnp.tile(cos, (1, LANES // cos.shape[1])), jnp.tile(sin, (1, LANES // sin.shape[1]))


def _att_rot_tables(S):
    cos, sin = _rot_angles(S, ATT_ROT_DIM, ROPE_THETA)
    half = ATT_ROT_DIM // 2
    gap = jnp.ones((S, ATT_PAIR_SHIFT - half), F32)
    cos_h = jnp.concatenate([cos, gap, cos, gap], axis=1)
    sin_h = jnp.concatenate([-sin, 0.0 * gap, sin, 0.0 * gap], axis=1)
    return cos_h, sin_h


def _att_head_order(w):
    half = ATT_ROT_DIM // 2
    k, n = w.shape
    wh = w.reshape(k, n // ATT_HEAD_DIM, ATT_HEAD_DIM)
    s = ATT_PAIR_SHIFT
    out = jnp.concatenate([wh[..., :half], wh[..., s:s + half], wh[..., 2 * half:s], wh[..., half:2 * half],
                           wh[..., s + half:]], axis=-1)
    return out.reshape(k, n)


def _rot_partner_cols(w, n_rot, period):
    half = n_rot // 2
    k, n = w.shape
    wh = w.reshape(k, n // period, period)
    out = jnp.concatenate([-wh[..., half:n_rot], wh[..., :half], jnp.zeros_like(wh[..., n_rot:])], axis=-1)
    return out.reshape(k, n)


def _in_proj_weights(w_in):
    o = np.cumsum([0, RET_QK_W, RET_QK_W, RET_V_W, RET_V_W, ATT_IN_W, SG_IN_W, GATE_IN_W])
    rqk, rv, rg, att, sg, gate = (w_in[:, o[0]:o[2]], w_in[:, o[2]:o[3]], w_in[:, o[3]:o[4]], w_in[:, o[4]:o[5]],
                                  w_in[:, o[5]:o[6]], w_in[:, o[6]:o[7]])
    att_blocks = []
    for blk in range(N_ATT_BLK):
        cols = att[:, blk * ATT_W:(blk + 1) * ATT_W]
        att_blocks.append(cols if blk % 3 == 2 else _att_head_order(cols))
    w = jnp.concatenate([gate, rv, rg, sg, rqk] + att_blocks, axis=1)
    return w.astype(BF16), _rot_partner_cols(rqk, RET_QK_DIM, RET_QK_DIM).astype(BF16)


def _in_proj(x2d, gain, w_bf16, wrot_bf16, tabs, S, tm):
    T = x2d.shape[0]
    nrep = S // tm
    tab_spec = pl.BlockSpec((tm, LANES), lambda i, j: (i % nrep, 0))

    return pl.pallas_call(
        _in_proj_kernel,
        grid=(T // tm, N_COL_BLK),
        in_specs=[
            pl.BlockSpec((tm, D_MODEL), lambda i, j: (i, 0)),
            pl.BlockSpec((1, D_MODEL), lambda i, j: (0, 0)),
            pl.BlockSpec((D_MODEL, COL_BLK), lambda i, j: (0, j)),
            pl.BlockSpec((D_MODEL, COL_BLK), lambda i, j: (0, 0)),
            tab_spec, tab_spec, tab_spec, tab_spec,
        ],
        out_specs=[
            pl.BlockSpec((tm, COL_BLK), lambda i, j: (i, jnp.minimum(j, N_PROJ_BLK - 1))),
            pl.BlockSpec((tm, ATTP_BLK), lambda i, j: (i, jnp.maximum(j - CB_ATT, 0))),
        ],
        out_shape=[
            jax.ShapeDtypeStruct((T, N_PROJ_BLK * COL_BLK), BF16),
            jax.ShapeDtypeStruct((T, N_ATT_BLK * ATTP_BLK), U32),
        ],
        scratch_shapes=[pltpu.VMEM((tm, D_MODEL), BF16)],
        compiler_params=_cparams(("arbitrary", "arbitrary")),
        name="in_proj",
    )(x2d, gain.reshape(1, D_MODEL), w_bf16, wrot_bf16, *tabs)


RET_TILE = 512
RET_CPT = RET_TILE // RET_CHUNK


def _retention_kernel(lg_ref, q_ref, k_ref, v_ref, g_ref, gn_ref, o_ref,
                      dmat_ref, tab_ref, sf_ref, sbrun_ref, sb_ref, *, n_tiles):
    b, ph, i = pl.program_id(0), pl.program_id(1), pl.program_id(2)
    C, H, DK, DV = RET_CHUNK, RET_HEADS, RET_QK_DIM, RET_V_DIM

    @pl.when(jnp.logical_and(b == 0, jnp.logical_and(ph == 0, i == 0)))
    def _():
        row = lax.broadcasted_iota(I32, (C, C), 0).astype(F32)
        col = lax.broadcasted_iota(I32, (C, C), 1).astype(F32)
        diff = row - col
        for h in range(H):
            fwd = jnp.exp(jnp.maximum(diff, 0.0) * lg_ref[0, h])
            bwd = jnp.exp(jnp.maximum(-diff, 0.0) * lg_ref[1, h])
            dmat_ref[h] = jnp.where(diff >= 0, fwd, bwd)
        pos = lax.broadcasted_iota(I32, (C, RET_QK_W), 0).astype(F32)
        head = lax.broadcasted_iota(I32, (C, RET_QK_W), 1) >> 6
        lgf = jnp.zeros((C, RET_QK_W), F32)
        lgb = jnp.zeros((C, RET_QK_W), F32)
        for h in range(H):
            lgf = jnp.where(head == h, lg_ref[0, h], lgf)
            lgb = jnp.where(head == h, lg_ref[1, h], lgb)
        tab_ref[0] = jnp.exp((C - 1.0 - pos) * lgf)
        tab_ref[1] = jnp.exp((pos + 1.0) * lgf)
        tab_ref[2] = jnp.exp(pos * lgb)
        tab_ref[3] = jnp.exp((C - pos) * lgb)
        tab_ref[4] = jnp.exp(C * lgf)
        tab_ref[5] = jnp.exp(C * lgb)

    def chunk_decay(idx):
        return tab_ref[idx][:DV, :].T

    def kv_state(kd, v):
        kdt = kd.T.astype(BF16)
        parts = [jnp.dot(kdt[h * DK:(h + 1) * DK, :], v[:, h * DV:(h + 1) * DV], preferred_element_type=F32)
                 for h in range(H)]
        return jnp.concatenate(parts, axis=0)

    @pl.when(ph == 0)
    def _():
        @pl.when(i == 0)
        def _():
            sbrun_ref[...] = jnp.zeros_like(sbrun_ref)

        tile = n_tiles - 1 - i
        dec = chunk_decay(5)
        for c in reversed(range(RET_CPT)):
            rows = pl.ds(c * C, C)
            sb_ref[tile * RET_CPT + c] = sbrun_ref[...]
            kd = k_ref[rows, :].astype(F32) * tab_ref[2]
            sbrun_ref[...] = dec * sbrun_ref[...] + kv_state(kd, v_ref[rows, :])

    @pl.when(ph == 1)
    def _():
        @pl.when(i == 0)
        def _():
            sf_ref[...] = jnp.zeros_like(sf_ref)

        dec = chunk_decay(4)
        head = lax.broadcasted_iota(I32, (C, RET_QK_W), 1) >> 6
        for c in range(RET_CPT):
            rows = pl.ds(c * C, C)
            q = q_ref[rows, :].astype(F32)
            k = k_ref[rows, :].astype(F32)
            v = v_ref[rows, :]
            kt = k.T.astype(BF16)
            qf = q * tab_ref[1]
            qb = q * tab_ref[3]
            sf = sf_ref[...].astype(BF16)
            sb = sb_ref[i * RET_CPT + c].astype(BF16)
            outs = []
            for h in range(H):
                sel = head == h
                qh = jnp.where(sel, q, 0.0).astype(BF16)
                s = jnp.dot(qh, kt, preferred_element_type=F32) * dmat_ref[h]
                o = jnp.dot(s.astype(BF16), v[:, h * DV:(h + 1) * DV], preferred_element_type=F32)
                o += jnp.dot(jnp.where(sel, qf, 0.0).astype(BF16), sf, preferred_element_type=F32)
                o += jnp.dot(jnp.where(sel, qb, 0.0).astype(BF16), sb, preferred_element_type=F32)
                mu = jnp.mean(o, axis=-1, keepdims=True)
                var = jnp.mean(jnp.square(o - mu), axis=-1, keepdims=True)
                outs.append((o - mu) * lax.rsqrt(var + NORM_EPS))
            y = jnp.concatenate(outs, axis=1) * gn_ref[...]
            o_ref[rows, :] = (y * g_ref[rows, :].astype(F32)).astype(BF16)
            sf_ref[...] = dec * sf_ref[...] + kv_state(k * tab_ref[0], v)


def _retention(proj, lg, gn_gain, B, S):
    nt = S // RET_TILE
    nc = S // RET_CHUNK
    qkw = RET_QK_W

    def kv_idx(blk):
        def f(b, ph, i, lg_ref):
            t = jnp.where(ph == 0, nt - 1 - i, i)
            return (b * nt + t, blk)
        return f

    def out_idx(blk):
        def f(b, ph, i, lg_ref):
            return (b * nt + i * ph, blk)
        return f

    grid_spec = pltpu.PrefetchScalarGridSpec(
        num_scalar_prefetch=1,
        grid=(B, 2, nt),
        in_specs=[
            pl.BlockSpec((RET_TILE, qkw), out_idx(2 * CB_RQK)),
            pl.BlockSpec((RET_TILE, qkw), kv_idx(2 * CB_RQK + 1)),
            pl.BlockSpec((RET_TILE, RET_V_W), kv_idx(CB_RV)),
            pl.BlockSpec((RET_TILE, RET_V_W), out_idx(CB_RG)),
            pl.BlockSpec((1, RET_V_W), lambda b, ph, i, lg_ref: (0, 0)),
        ],
        out_specs=pl.BlockSpec((RET_TILE, RET_V_W), out_idx(0)),
        scratch_shapes=[
            pltpu.VMEM((RET_HEADS, RET_CHUNK, RET_CHUNK), F32),
            pltpu.VMEM((6, RET_CHUNK, qkw), F32),
            pltpu.VMEM((qkw, RET_V_DIM), F32),
            pltpu.VMEM((qkw, RET_V_DIM), F32),
            pltpu.VMEM((nc, qkw, RET_V_DIM), F32),
        ],
    )
    return pl.pallas_call(
        functools.partial(_retention_kernel, n_tiles=nt),
        grid_spec=grid_spec,
        out_shape=jax.ShapeDtypeStruct((B * S, RET_V_W), BF16),
        compiler_params=_cparams(("arbitrary", "arbitrary", "arbitrary")),
        name="retention",
    )(lg, proj, proj, proj, proj, gn_gain.reshape(1, RET_V_W))


ATT_QB = 128
LOG2E = 1.4426950408889634


def _dil_attn_kernel(q_ref, km_ref, kp_ref, kn_ref, vm_ref, vp_ref, vn_ref, o_ref, lse_ref, *, L, Lq, d):
    i = pl.program_id(1)
    R, QB, HD = ATT_R, ATT_QB, ATT_HEAD_DIM
    nq = Lq // QB
    NK = QB + 2 * R
    row = lax.broadcasted_iota(I32, (QB, NK), 0)
    col = lax.broadcasted_iota(I32, (QB, NK), 1)
    band = jnp.logical_and(col - row >= 0, col - row <= 2 * R)
    lane = lax.broadcasted_iota(I32, (QB, LANES), 1)
    scale = HD ** -0.5

    def rows(ref, r, n):
        return ref[pl.ds(r, n, stride=d), :] if d > 1 else ref[...]

    def residue(r):
        q_all = rows(q_ref, r, Lq)
        k_all = jnp.concatenate([rows(kp_ref, r, R), rows(km_ref, r, Lq), rows(kn_ref, r, R)], axis=0)
        v_all = jnp.concatenate([rows(vp_ref, r, R), rows(vm_ref, r, Lq), rows(vn_ref, r, R)], axis=0)
        for qb in range(nq):
            base = i * Lq + qb * QB - R
            ok = jnp.logical_and(band, jnp.logical_and(col >= -base, col < L - base))
            q2 = _unpack_halves(q_all[qb * QB:(qb + 1) * QB])
            k2 = _unpack_halves(k_all[qb * QB:qb * QB + NK])
            v2 = _unpack_halves(v_all[qb * QB:qb * QB + NK])
            outs, lses = [], []
            for half in range(2):
                s = lax.dot_general(q2[half].astype(BF16), k2[half].astype(BF16), (((1,), (1,)), ((), ())),
                                    preferred_element_type=F32)
                s = jnp.where(ok, s, NEG_BIG)
                m = jnp.max(s, axis=-1, keepdims=True)
                p = jnp.exp2((s - m) * (scale * LOG2E))
                den = jnp.sum(p, axis=-1, keepdims=True)
                pn = (p * (1.0 / den)).astype(BF16)
                outs.append(jnp.dot(pn, v2[half].astype(BF16), preferred_element_type=F32))
                lses.append(m * scale + jnp.log(den))
            packed = _pack_halves(jnp.concatenate(outs, axis=1))
            lse_tile = jnp.where(lane < LANES // 2, lses[0], lses[1])
            dst = pl.ds(r + qb * QB * d, QB, stride=d) if d > 1 else pl.ds(qb * QB, QB)
            o_ref[dst, :] = packed
            lse_ref[dst, :] = lse_tile

    if d > 1:
        def body(r, carry):
            residue(r)
            return carry

        lax.fori_loop(0, d, body, 0, unroll=4)
    else:
        residue(0)


def _dil_attn(attp, g, B, S):
    window, d = ATT_PATTERNS[g]
    L = S // d
    Lq = min(L, 512 if d == 1 else 256)
    tb = Lq * d
    nb = S // tb
    halo = ATT_R * d
    hb = tb // halo
    nhb = S // halo
    npair = ATTP_BLK // LANES
    cq, ck, cv = (npair * (3 * g + kind) for kind in range(3))

    def main(c):
        return pl.BlockSpec((tb, LANES), lambda b, i, p: (b * nb + i, c + p))

    def prev(c):
        return pl.BlockSpec((halo, LANES), lambda b, i, p: (b * nhb + jnp.maximum(i * hb - 1, 0), c + p))

    def nxt(c):
        return pl.BlockSpec((halo, LANES), lambda b, i, p: (b * nhb + jnp.minimum((i + 1) * hb, nhb - 1), c + p))

    out_spec = pl.BlockSpec((tb, LANES), lambda b, i, p: (b * nb + i, p))
    return pl.pallas_call(
        functools.partial(_dil_attn_kernel, L=L, Lq=Lq, d=d),
        grid=(B, nb, npair),
        in_specs=[main(cq), main(ck), prev(ck), nxt(ck), main(cv), prev(cv), nxt(cv)],
        out_specs=[out_spec, out_spec],
        out_shape=[
            jax.ShapeDtypeStruct((B * S, npair * LANES), U32),
            jax.ShapeDtypeStruct((B * S, npair * LANES), F32),
        ],
        compiler_params=_cparams(("arbitrary", "arbitrary", "arbitrary")),
        name=f"dil_attn_{g}",
    )(attp, attp, attp, attp, attp, attp, attp)


MERGE_TILE = 512


def _merge_kernel(x_ref, r_ref, o0_ref, o1_ref, o2_ref, l0_ref, l1_ref, l2_ref, su_ref, sv_ref,
                  g0_ref, g1_ref, g2_ref, wr_ref, wa_ref, ws_ref, wo_ref, sgw_ref, sgb_ref, lng_ref,
                  nf_ref, wrt_ref, brt_ref, tri_ref,
                  x2_ref, hp_ref, route_ref, cnt_ref, carry_ref):
    i = pl.program_id(0)
    tm = x_ref.shape[0]
    HD = ATT_HEAD_DIM

    @pl.when(i == 0)
    def _():
        carry_ref[...] = jnp.zeros_like(carry_ref)

    l0, l1, l2 = l0_ref[...], l1_ref[...], l2_ref[...]
    m = jnp.maximum(jnp.maximum(l0, l1), l2)
    e0, e1, e2 = jnp.exp(l0 - m), jnp.exp(l1 - m), jnp.exp(l2 - m)
    inv = 1.0 / (e0 + e1 + e2)
    wts = (e0 * inv, e1 * inv, e2 * inv)
    outs = [jnp.concatenate(_unpack_halves(o_ref[...]), axis=1) for o_ref in (o0_ref, o1_ref, o2_ref)]
    heads = []
    for h in range(ATT_HEADS):
        cols = slice(h * HD, (h + 1) * HD)
        c = (h % 2) * LANES + (h // 2) * (LANES // 2)
        heads.append(sum(w[:, c:c + 1] * o[:, cols] for w, o in zip(wts, outs)))
    att = jnp.concatenate(heads, axis=1).astype(BF16)
    y_att = jnp.dot(att, wa_ref[...], preferred_element_type=F32)
    y_ret = jnp.dot(r_ref[...], wr_ref[...], preferred_element_type=F32)

    u = su_ref[...].astype(F32)
    v = sv_ref[...].astype(F32)
    mu = jnp.mean(v, axis=-1, keepdims=True)
    var = jnp.mean(jnp.square(v - mu), axis=-1, keepdims=True)
    vn = ((v - mu) * lax.rsqrt(var + NORM_EPS) * lng_ref[...]).astype(BF16)
    gw = SG_WIDTH // SG_GROUPS
    chunks = []
    for c in range(tm // SG_CHUNK):
        rows = slice(c * SG_CHUNK, (c + 1) * SG_CHUNK)
        parts = [jnp.dot(sgw_ref[g], vn[rows, g * gw:(g + 1) * gw], preferred_element_type=F32)
                 for g in range(SG_GROUPS)]
        chunks.append(jnp.concatenate(parts, axis=1) + sgb_ref[...])
    vs = jnp.concatenate(chunks, axis=0)
    y_sg = jnp.dot((u * vs).astype(BF16), ws_ref[...], preferred_element_type=F32)

    merged = (g0_ref[...].astype(F32) * y_ret + g1_ref[...].astype(F32) * y_att
              + g2_ref[...].astype(F32) * y_sg)
    x2 = x_ref[...] + jnp.dot(merged.astype(BF16), wo_ref[...], preferred_element_type=F32)
    x2_ref[...] = x2

    ms = jnp.mean(x2 * x2, axis=-1, keepdims=True)
    h2 = x2 * lax.rsqrt(ms + NORM_EPS) * nf_ref[...]
    hp_ref[...] = _pack_halves(h2)

    h_hi = h2.astype(BF16)
    h_lo = (h2 - h_hi.astype(F32)).astype(BF16)
    hw = jnp.dot(h_hi, wrt_ref[...], preferred_element_type=F32)
    logits = (hw[:, :LANES] + hw[:, LANES:] + jnp.dot(h_lo, wrt_ref[:, :LANES], preferred_element_type=F32)
              + brt_ref[...])
    lane = lax.broadcasted_iota(I32, logits.shape, 1)
    big = jnp.int32(LANES)
    is_g = lane < N_GROUPS
    gl = jnp.where(is_g, logits, -jnp.inf)
    gmax = jnp.max(gl, axis=-1, keepdims=True)
    g_idx = jnp.min(jnp.where(jnp.logical_and(is_g, gl == gmax), lane, big), axis=-1, keepdims=True)
    g_w = 1.0 / jnp.sum(jnp.where(is_g, jnp.exp(gl - gmax), 0.0), axis=-1, keepdims=True)
    lo = ROUTE_LANE0 + g_idx * EXP_PER_GROUP
    in_grp = jnp.logical_and(lane >= lo, lane < lo + EXP_PER_GROUP)
    el = jnp.where(in_grp, logits, -jnp.inf)
    v1 = jnp.max(el, axis=-1, keepdims=True)
    i1 = jnp.min(jnp.where(el == v1, lane, big), axis=-1, keepdims=True)
    el2 = jnp.where(lane == i1, -jnp.inf, el)
    v2 = jnp.max(el2, axis=-1, keepdims=True)
    i2 = jnp.min(jnp.where(el2 == v2, lane, big), axis=-1, keepdims=True)
    t = jnp.exp(v2 - v1)
    w1 = g_w / (1.0 + t)
    w2 = g_w * t / (1.0 + t)

    oh1 = lane == i1
    oh2 = lane == i2
    both = jnp.where(jnp.logical_or(oh1, oh2), 1.0, 0.0)
    before = jnp.dot(tri_ref[...], both.astype(BF16), preferred_element_type=F32) + carry_ref[0:1, :]
    rank1 = jnp.sum(jnp.where(oh1, before, 0.0), axis=-1, keepdims=True)
    rank2 = jnp.sum(jnp.where(oh2, before, 0.0), axis=-1, keepdims=True)
    carry = carry_ref[0:1, :] + jnp.sum(both, axis=0, keepdims=True)
    carry_ref[...] = jnp.broadcast_to(carry, carry_ref.shape)
    cnt_ref[...] = jnp.broadcast_to(carry, cnt_ref.shape)

    out = jnp.where(lane == 0, (i1 - ROUTE_LANE0).astype(F32), 0.0)
    out = jnp.where(lane == 1, (i2 - ROUTE_LANE0).astype(F32), out)
    out = jnp.where(lane == 2, w1, out)
    out = jnp.where(lane == 3, w2, out)
    out = jnp.where(lane == 4, rank1, out)
    out = jnp.where(lane == 5, rank2, out)
    route_ref[...] = out


def _merge(x2d, r, os_, lses, proj, lw):
    T = x2d.shape[0]
    tm = MERGE_TILE
    row = lambda w: pl.BlockSpec((tm, w), lambda i: (i, 0))
    pcol = lambda w, cb: pl.BlockSpec((tm, w), lambda i: (i, cb))
    full = lambda a: pl.BlockSpec(a.shape, lambda i: (0,) * a.ndim)
    gate_cb = CB_GATE * COL_BLK // D_MODEL
    weights = [lw["w_ret_out"], lw["w_att_out"], lw["w_sg_out"], lw["w_o"], lw["sg_w"], lw["sg_b_full"],
               lw["sg_ln_gain"], lw["norm_ffn"], lw["w_router"], lw["b_router"], lw["tri"]]
    return pl.pallas_call(
        _merge_kernel,
        grid=(T // tm,),
        in_specs=[row(D_MODEL), row(RET_V_W)] + [row(2 * LANES)] * (2 * len(ATT_PATTERNS)) + [
                  pcol(SG_WIDTH, CB_SGU), pcol(SG_WIDTH, CB_SGV),
                  pcol(D_MODEL, gate_cb), pcol(D_MODEL, gate_cb + 1), pcol(D_MODEL, gate_cb + 2)
                 ] + [full(a) for a in weights],
        out_specs=[row(D_MODEL), row(D_MODEL // 2), row(LANES), pl.BlockSpec((8, LANES), lambda i: (0, 0))],
        out_shape=[jax.ShapeDtypeStruct((T, D_MODEL), F32), jax.ShapeDtypeStruct((T, D_MODEL // 2), U32),
                   jax.ShapeDtypeStruct((T, LANES), F32), jax.ShapeDtypeStruct((8, LANES), F32)],
        scratch_shapes=[pltpu.VMEM((8, LANES), F32)],
        compiler_params=_cparams(("arbitrary",)),
        name="merge",
    )(x2d, r, *os_, *lses, proj, proj, proj, proj, proj, *weights)


DISPATCH_TILE = 1024
ISSUE_UNROLL = 8


def _dispatch_kernel(dest_ref, h_ref, xb_in_ref, xb_ref, sem):
    del xb_in_ref
    tm = DISPATCH_TILE

    def issue(t, carry):
        src = h_ref.at[pl.ds(t, 1)]
        for k in range(TOP_K):
            pltpu.make_async_copy(src, xb_ref.at[pl.ds(dest_ref[0, 0, TOP_K * t + k], 1)], sem).start(priority=k)
        return carry

    lax.fori_loop(0, tm, issue, 0, unroll=ISSUE_UNROLL)
    pltpu.make_async_copy(xb_ref.at[pl.ds(0, TOP_K * tm)], xb_ref.at[pl.ds(0, TOP_K * tm)], sem).wait()


def _dispatch(hp, dest, P):
    T = hp.shape[0]
    tm = DISPATCH_TILE
    half = D_MODEL // 2
    return pl.pallas_call(
        _dispatch_kernel,
        grid=(T // tm,),
        in_specs=[
            pl.BlockSpec((1, 1, TOP_K * tm), lambda i: (i, 0, 0), memory_space=pltpu.SMEM),
            pl.BlockSpec((tm, half), lambda i: (i, 0)),
            pl.BlockSpec(memory_space=pl.ANY),
        ],
        out_specs=pl.BlockSpec(memory_space=pl.ANY),
        out_shape=jax.ShapeDtypeStruct((P, half), U32),
        scratch_shapes=[pltpu.SemaphoreType.DMA(())],
        input_output_aliases={2: 0},
        compiler_params=_cparams(("arbitrary",)),
        name="dispatch",
    )(dest.reshape(T // tm, 1, TOP_K * tm), hp, jnp.zeros((P, half), U32))


def _experts_kernel(blk_e_ref, nused_ref, x_ref, wg_ref, wu_ref, wd_ref, y_ref, wg_s, wu_s, wd_s):
    i = pl.program_id(0)

    @pl.when(jnp.logical_or(i == 0, blk_e_ref[i] != blk_e_ref[jnp.maximum(i - 1, 0)]))
    def _():
        wg_s[...] = wg_ref[0, 0].astype(BF16)
        wu_s[...] = wu_ref[0, 0].astype(BF16)
        wd_s[...] = wd_ref[0, 0].astype(BF16)

    @pl.when(i < nused_ref[0])
    def _():
        x = jnp.concatenate(_unpack_halves(x_ref[...]), axis=1).astype(BF16)
        gate = jnp.dot(x, wg_s[...], preferred_element_type=F32)
        up = jnp.dot(x, wu_s[...], preferred_element_type=F32)
        hid = (gate * _sigmoid(gate) * up).astype(BF16)
        y_ref[...] = jnp.dot(hid, wd_s[...], preferred_element_type=F32)

    @pl.when(i >= nused_ref[0])
    def _():
        y_ref[...] = jnp.zeros_like(y_ref)


def _experts(xb, blk_e, nused, wg, wu, wd, layer):
    P = xb.shape[0]
    bm = MOE_BLOCK
    grid_spec = pltpu.PrefetchScalarGridSpec(
        num_scalar_prefetch=2,
        grid=(P // bm,),
        in_specs=[
            pl.BlockSpec((bm, D_MODEL // 2), lambda i, e, n: (i, 0)),
            pl.BlockSpec((1, 1, D_MODEL, EXP_HIDDEN), lambda i, e, n: (layer, e[i], 0, 0)),
            pl.BlockSpec((1, 1, D_MODEL, EXP_HIDDEN), lambda i, e, n: (layer, e[i], 0, 0)),
            pl.BlockSpec((1, 1, EXP_HIDDEN, D_MODEL), lambda i, e, n: (layer, e[i], 0, 0)),
        ],
        out_specs=pl.BlockSpec((bm, D_MODEL), lambda i, e, n: (i, 0)),
        scratch_shapes=[pltpu.VMEM((D_MODEL, EXP_HIDDEN), BF16), pltpu.VMEM((D_MODEL, EXP_HIDDEN), BF16),
                        pltpu.VMEM((EXP_HIDDEN, D_MODEL), BF16)],
    )
    return pl.pallas_call(
        _experts_kernel,
        grid_spec=grid_spec,
        out_shape=jax.ShapeDtypeStruct((P, D_MODEL), F32),
        compiler_params=_cparams(("arbitrary",)),
        name="experts",
    )(blk_e, nused, xb, wg, wu, wd)


COMBINE_TILE = 256


def _combine_kernel(dest_ref, dest_next_ref, x_ref, route_ref, gain_ref, y_ref, o_ref, buf_ref, sems, *, final_norm):
    i = pl.program_id(0)
    n = pl.num_programs(0)
    tm = COMBINE_TILE

    def gather(idx_ref, slot):
        def issue(t, carry):
            for k in range(TOP_K):
                pltpu.make_async_copy(y_ref.at[pl.ds(idx_ref[0, 0, TOP_K * t + k], 1)],
                                      buf_ref.at[slot, k, pl.ds(t, 1)], sems.at[slot]).start(priority=k)
            return carry

        lax.fori_loop(0, tm, issue, 0, unroll=ISSUE_UNROLL)

    slot = i % 2

    @pl.when(i == 0)
    def _():
        gather(dest_ref, 0)

    @pl.when(i + 1 < n)
    def _():
        gather(dest_next_ref, 1 - slot)

    pltpu.make_async_copy(buf_ref.at[slot], buf_ref.at[slot], sems.at[slot]).wait()
    route = route_ref[...]
    x = x_ref[...] + route[:, 2:3] * buf_ref[slot, 0] + route[:, 3:4] * buf_ref[slot, 1]
    if final_norm:
        ms = jnp.mean(x * x, axis=-1, keepdims=True)
        x = x * lax.rsqrt(ms + NORM_EPS) * gain_ref[...]
    o_ref[...] = x


def _combine(x2, route, yb, dest, gain, final_norm):
    T = x2.shape[0]
    tm = COMBINE_TILE
    nt = T // tm
    dest3 = dest.reshape(nt, 1, TOP_K * tm)
    return pl.pallas_call(
        functools.partial(_combine_kernel, final_norm=final_norm),
        grid=(nt,),
        in_specs=[
            pl.BlockSpec((1, 1, TOP_K * tm), lambda i: (i, 0, 0), memory_space=pltpu.SMEM),
            pl.BlockSpec((1, 1, TOP_K * tm), lambda i: (jnp.minimum(i + 1, nt - 1), 0, 0), memory_space=pltpu.SMEM),
            pl.BlockSpec((tm, D_MODEL), lambda i: (i, 0)),
            pl.BlockSpec((tm, LANES), lambda i: (i, 0)),
            pl.BlockSpec((1, D_MODEL), lambda i: (0, 0)),
            pl.BlockSpec(memory_space=pl.ANY),
        ],
        out_specs=pl.BlockSpec((tm, D_MODEL), lambda i: (i, 0)),
        out_shape=jax.ShapeDtypeStruct((T, D_MODEL), F32),
        scratch_shapes=[pltpu.VMEM((2, TOP_K, tm, D_MODEL), F32), pltpu.SemaphoreType.DMA((2,))],
        compiler_params=_cparams(("arbitrary",)),
        name="combine_final" if final_norm else "combine",
    )(dest3, dest3, x2, route, gain.reshape(1, D_MODEL), yb)


def _layer_weights(l, p):
    w_router = jnp.zeros((D_MODEL, LANES), F32)
    w_router = w_router.at[:, :N_GROUPS].set(p["w_router_group"][l])
    w_router = w_router.at[:, ROUTE_LANE0:ROUTE_LANE0 + N_EXPERTS].set(p["w_router_expert"][l])
    b_router = jnp.zeros((1, LANES), F32)
    b_router = b_router.at[0, :N_GROUPS].set(p["b_router_group"][l])
    b_router = b_router.at[0, ROUTE_LANE0:ROUTE_LANE0 + N_EXPERTS].set(p["b_router_expert"][l])
    tm = MERGE_TILE
    tri = (jnp.arange(tm)[:, None] > jnp.arange(tm)[None, :]).astype(BF16)
    gw = SG_WIDTH // SG_GROUPS
    w_in, w_in_rot = _in_proj_weights(p["w_in"][l])
    return {
        "norm_mix": p["norm_mix"][l],
        "w_in": w_in,
        "w_in_rot": w_in_rot,
        "lg": jnp.stack([-jnp.exp(p["ret_decay_fwd"][l].astype(F32)), -jnp.exp(p["ret_decay_bwd"][l].astype(F32))]),
        "ret_gn_gain": p["ret_gn_gain"][l],
        "w_ret_out": p["w_ret_out"][l].astype(BF16),
        "w_att_out": p["w_att_out"][l].astype(BF16),
        "w_sg_out": p["w_sg_out"][l].astype(BF16),
        "w_o": p["w_o"][l].astype(BF16),
        "sg_w": p["sg_w"][l].astype(BF16),
        "sg_b_full": jnp.repeat(p["sg_b"][l].T, gw, axis=1),
        "sg_ln_gain": p["sg_ln_gain"][l].reshape(1, SG_WIDTH),
        "norm_ffn": p["norm_ffn"][l].reshape(1, D_MODEL),
        "w_router": jnp.concatenate([w_router.astype(BF16),
                                     (w_router - w_router.astype(BF16).astype(F32)).astype(BF16)], axis=1),
        "b_router": b_router,
        "tri": tri,
    }


def _moe_plan(route, cnt, T):
    bm = MOE_BLOCK
    eid = route[:, 0:TOP_K].astype(I32)
    rank = route[:, 4:4 + TOP_K].astype(I32)
    counts = cnt[0, ROUTE_LANE0:ROUTE_LANE0 + N_EXPERTS].astype(I32)
    padded = (counts + bm - 1) // bm * bm
    pad_end = jnp.cumsum(padded)
    pad_start = pad_end - padded
    experts = jnp.arange(N_EXPERTS, dtype=I32)
    dest = jnp.sum(jnp.where(eid[..., None] == experts, pad_start, 0), axis=-1) + rank
    P = T * TOP_K + N_EXPERTS * bm
    nblk = P // bm
    blk_start = jnp.arange(nblk, dtype=I32) * bm
    blk_e = jnp.minimum(jnp.sum((pad_end[None, :] <= blk_start[:, None]).astype(I32), axis=1), N_EXPERTS - 1)
    nused = (pad_end[-1:] // bm).astype(I32)
    return dest.reshape(-1), blk_e, nused, P


def _run_trunk(x, lws, w_exp, tabs, norm_final):
    B, S, _ = x.shape
    T = B * S
    x2d = x.reshape(T, D_MODEL)
    tm_in = min(2048, S)
    for l, lw in enumerate(lws):
        proj, attp = _in_proj(x2d, lw["norm_mix"], lw["w_in"], lw["w_in_rot"], tabs, S, tm_in)
        r = _retention(proj, lw["lg"], lw["ret_gn_gain"], B, S)
        os_, lses = zip(*[_dil_attn(attp, g, B, S) for g in range(len(ATT_PATTERNS))])
        x2, hp, route, cnt = _merge(x2d, r, os_, lses, proj, lw)
        dest, blk_e, nused, P = _moe_plan(route, cnt, T)
        xb = _dispatch(hp, dest, P)
        yb = _experts(xb, blk_e, nused, *w_exp, layer=l)
        x2d = _combine(x2, route, yb, dest, norm_final, final_norm=(l == len(lws) - 1))
    return x2d.reshape(B, S, D_MODEL)


def kernel(x_prompt, x_sample, norm_mix, w_in, ret_decay_fwd, ret_decay_bwd, ret_gn_gain, sg_ln_gain, sg_w, sg_b,
           w_ret_out, w_att_out, w_sg_out, w_o, norm_ffn, w_router_group, b_router_group, w_router_expert,
           b_router_expert, w_exp_gate, w_exp_up, w_exp_down, norm_final):
    p = dict(norm_mix=norm_mix, w_in=w_in, ret_decay_fwd=ret_decay_fwd, ret_decay_bwd=ret_decay_bwd,
             ret_gn_gain=ret_gn_gain, sg_ln_gain=sg_ln_gain, sg_w=sg_w, sg_b=sg_b, w_ret_out=w_ret_out,
             w_att_out=w_att_out, w_sg_out=w_sg_out, w_o=w_o, norm_ffn=norm_ffn, w_router_group=w_router_group,
             b_router_group=b_router_group, w_router_expert=w_router_expert, b_router_expert=b_router_expert,
             w_exp_gate=w_exp_gate, w_exp_up=w_exp_up, w_exp_down=w_exp_down)
    lws = [_layer_weights(l, p) for l in range(DEPTH)]
    s_max = max(x_prompt.shape[1], x_sample.shape[1])
    tabs = _ret_rot_tables(s_max) + _att_rot_tables(s_max)
    w_exp = (w_exp_gate, w_exp_up, w_exp_down)
    y_prompt = _run_trunk(x_prompt, lws, w_exp, tabs, norm_final)
    y_sample = _run_trunk(x_sample, lws, w_exp, tabs, norm_final)
    return (y_prompt, y_sample)
```

```python
import functools

import jax
import jax.numpy as jnp
import numpy as np
from jax import lax
from jax.experimental import pallas as pl
from jax.experimental.pallas import tpu as pltpu

F32 = jnp.float32
BF16 = jnp.bfloat16
U32 = jnp.uint32
I32 = jnp.int32

D_MODEL = 1024
DEPTH = 2
NORM_EPS = 1e-6
RET_HEADS = 4
RET_QK_DIM = 64
RET_V_DIM = 128
RET_CHUNK = 128
RET_THETA = 10000.0
ATT_PATTERNS = ((128, 1), (512, 4), (2048, 16))
ATT_HEADS = 4
ATT_HEAD_DIM = 128
ATT_ROT_DIM = ATT_HEAD_DIM // 4
ROPE_THETA = 500000.0
SG_GROUPS = 4
SG_CHUNK = 128
SG_WIDTH = 512
N_BRANCH = 3
N_GROUPS = 4
EXP_PER_GROUP = 8
N_EXPERTS = N_GROUPS * EXP_PER_GROUP
TOP_K = 2
EXP_HIDDEN = 512
RET_QK_W = RET_HEADS * RET_QK_DIM
RET_V_W = RET_HEADS * RET_V_DIM
ATT_W = ATT_HEADS * ATT_HEAD_DIM
ATT_IN_W = len(ATT_PATTERNS) * 3 * ATT_W
SG_IN_W = 2 * SG_WIDTH
GATE_IN_W = N_BRANCH * D_MODEL
N_IN = 2 * RET_QK_W + 2 * RET_V_W + ATT_IN_W + SG_IN_W + GATE_IN_W
NEG_BIG = -1e30

LANES = 128
COL_BLK = 512
N_COL_BLK = N_IN // COL_BLK
ATT_R = 64
assert all(w // (2 * d) == ATT_R for w, d in ATT_PATTERNS)
CB_GATE, CB_RV, CB_RG, CB_SGU, CB_SGV, CB_RQK, CB_ATT = 0, 6, 7, 8, 9, 10, 11
N_PROJ_BLK = CB_ATT
N_ATT_BLK = N_COL_BLK - CB_ATT
ATT_PAIR_SHIFT = LANES // 2
ATTP_BLK = COL_BLK // 2
ROUTE_LANE0 = N_GROUPS
MOE_BLOCK = 512
VMEM_LIMIT = 56 * 1024 * 1024
HI_HALF = 0xFFFF0000
IN_PROJ_ROWS = 512


def _cparams(sem, vmem=VMEM_LIMIT):
    return pltpu.CompilerParams(dimension_semantics=sem, vmem_limit_bytes=vmem)


def _sigmoid(x):
    return 0.5 * jnp.tanh(0.5 * x) + 0.5


def _pack_halves(y):
    n = y.shape[1] // 2
    bits = lax.bitcast_convert_type(y.astype(BF16).astype(F32), U32)
    return (bits[:, :n] >> 16) | (bits[:, n:] & jnp.uint32(HI_HALF))


def _unpack_halves(p):
    return (lax.bitcast_convert_type(p << 16, F32), lax.bitcast_convert_type(p & jnp.uint32(HI_HALF), F32))


def _in_proj_kernel(x_ref, g_ref, w_ref, wrot_ref, rc_ref, rs_ref, ac_ref, as_ref, o_ref, oa_ref, h_ref):
    j = pl.program_id(1)

    chunks = [pl.ds(r * IN_PROJ_ROWS, IN_PROJ_ROWS) for r in range(x_ref.shape[0] // IN_PROJ_ROWS)]

    @pl.when(j == 0)
    def _():
        for rows in chunks:
            x = x_ref[rows, :]
            ms = jnp.mean(x * x, axis=-1, keepdims=True)
            h = (x * lax.rsqrt(ms + NORM_EPS) * g_ref[...]).astype(BF16)
            h_ref[rows, :] = h
            o_ref[rows, :] = _sigmoid(jnp.dot(h, w_ref[...], preferred_element_type=F32)).astype(BF16)

    def dot(w, rows):
        return jnp.dot(h_ref[rows, :], w[...], preferred_element_type=F32)

    reps = COL_BLK // LANES

    def tiled(tab_ref, rows):
        return jnp.concatenate([tab_ref[rows, :]] * reps, axis=1)

    att_kind = (j - CB_ATT) % 3

    @pl.when(jnp.logical_and(j > 0, j < CB_RV))
    def _():
        for rows in chunks:
            o_ref[rows, :] = _sigmoid(dot(w_ref, rows)).astype(BF16)

    @pl.when(j == CB_RV)
    def _():
        for rows in chunks:
            o_ref[rows, :] = dot(w_ref, rows).astype(BF16)

    @pl.when(j == CB_RG)
    def _():
        for rows in chunks:
            acc = dot(w_ref, rows)
            o_ref[rows, :] = (acc * _sigmoid(acc)).astype(BF16)

    @pl.when(jnp.logical_or(j == CB_SGU, j == CB_SGV))
    def _():
        for rows in chunks:
            o_ref[rows, :] = jax.nn.gelu(dot(w_ref, rows)).astype(BF16)

    @pl.when(j == CB_RQK)
    def _():
        for rows in chunks:
            y = dot(w_ref, rows) * tiled(rc_ref, rows) + dot(wrot_ref, rows) * tiled(rs_ref, rows)
            lane = lax.broadcasted_iota(I32, y.shape, 1)
            y = jnp.where(lane >= RET_QK_W, y * (RET_QK_DIM ** -0.5), y)
            o_ref[rows, :] = y.astype(BF16)

    @pl.when(jnp.logical_and(j >= CB_ATT, att_kind != 2))
    def _():
        for rows in chunks:
            acc = dot(w_ref, rows)
            partner = jnp.concatenate([pltpu.roll(acc[:, c * LANES:(c + 1) * LANES], ATT_PAIR_SHIFT, 1)
                                       for c in range(reps)], axis=1)
            oa_ref[rows, :] = _pack_halves(acc * tiled(ac_ref, rows) + partner * tiled(as_ref, rows))

    @pl.when(jnp.logical_and(j >= CB_ATT, att_kind == 2))
    def _():
        for rows in chunks:
            oa_ref[rows, :] = _pack_halves(dot(w_ref, rows))


def _rot_angles(S, n_rot, theta):
    half = n_rot // 2
    inv = jnp.power(jnp.float32(theta), -jnp.arange(half, dtype=F32) * (2.0 / n_rot))
    ang = jnp.arange(S, dtype=F32)[:, None] * inv[None, :]
    return jnp.cos(ang), jnp.sin(ang)


def _ret_rot_tables(S):
    cos, sin = _rot_angles(S, RET_QK_DIM, RET_THETA)
    return jnp.tile(cos, (1, LANES // cos.shape[1])), jnp.tile(sin, (1, LANES // sin.shape[1]))


def _att_rot_tables(S):
    cos, sin = _rot_angles(S, ATT_ROT_DIM, ROPE_THETA)
    half = ATT_ROT_DIM // 2
    gap = jnp.ones((S, ATT_PAIR_SHIFT - half), F32)
    cos_h = jnp.concatenate([cos, gap, cos, gap], axis=1)
    sin_h = jnp.concatenate([-sin, 0.0 * gap, sin, 0.0 * gap], axis=1)
    return cos_h, sin_h


def _att_head_order(w):
    half = ATT_ROT_DIM // 2
    k, n = w.shape
    wh = w.reshape(k, n // ATT_HEAD_DIM, ATT_HEAD_DIM)
    s = ATT_PAIR_SHIFT
    out = jnp.concatenate([wh[..., :half], wh[..., s:s + half], wh[..., 2 * half:s], wh[..., half:2 * half],
                           wh[..., s + half:]], axis=-1)
    return out.reshape(k, n)


def _rot_partner_cols(w, n_rot, period):
    half = n_rot // 2
    k, n = w.shape
    wh = w.reshape(k, n // period, period)
    out = jnp.concatenate([-wh[..., half:n_rot], wh[..., :half], jnp.zeros_like(wh[..., n_rot:])], axis=-1)
    return out.reshape(k, n)


def _in_proj_weights(w_in):
    o = np.cumsum([0, RET_QK_W, RET_QK_W, RET_V_W, RET_V_W, ATT_IN_W, SG_IN_W, GATE_IN_W])
    rqk, rv, rg, att, sg, gate = (w_in[:, o[0]:o[2]], w_in[:, o[2]:o[3]], w_in[:, o[3]:o[4]], w_in[:, o[4]:o[5]],
                                  w_in[:, o[5]:o[6]], w_in[:, o[6]:o[7]])
    att_blocks = []
    for blk in range(N_ATT_BLK):
        cols = att[:, blk * ATT_W:(blk + 1) * ATT_W]
        att_blocks.append(cols if blk % 3 == 2 else _att_head_order(cols))
    w = jnp.concatenate([gate, rv, rg, sg, rqk] + att_blocks, axis=1)
    return w.astype(BF16), _rot_partner_cols(rqk, RET_QK_DIM, RET_QK_DIM).astype(BF16)


def _in_proj(x2d, gain, w_bf16, wrot_bf16, tabs, S, tm):
    T = x2d.shape[0]
    nrep = S // tm
    tab_spec = pl.BlockSpec((tm, LANES), lambda i, j: (i % nrep, 0))

    return pl.pallas_call(
        _in_proj_kernel,
        grid=(T // tm, N_COL_BLK),
        in_specs=[
            pl.BlockSpec((tm, D_MODEL), lambda i, j: (i, 0)),
            pl.BlockSpec((1, D_MODEL), lambda i, j: (0, 0)),
            pl.BlockSpec((D_MODEL, COL_BLK), lambda i, j: (0, j)),
            pl.BlockSpec((D_MODEL, COL_BLK), lambda i, j: (0, 0)),
            tab_spec, tab_spec, tab_spec, tab_spec,
        ],
        out_specs=[
            pl.BlockSpec((tm, COL_BLK), lambda i, j: (i, jnp.minimum(j, N_PROJ_BLK - 1))),
            pl.BlockSpec((tm, ATTP_BLK), lambda i, j: (i, jnp.maximum(j - CB_ATT, 0))),
        ],
        out_shape=[
            jax.ShapeDtypeStruct((T, N_PROJ_BLK * COL_BLK), BF16),
            jax.ShapeDtypeStruct((T, N_ATT_BLK * ATTP_BLK), U32),
        ],
        scratch_shapes=[pltpu.VMEM((tm, D_MODEL), BF16)],
        compiler_params=_cparams(("arbitrary", "arbitrary")),
        name="in_proj",
    )(x2d, gain.reshape(1, D_MODEL), w_bf16, wrot_bf16, *tabs)


RET_TILE = 512
RET_CPT = RET_TILE // RET_CHUNK


def _retention_kernel(lg_ref, q_ref, k_ref, v_ref, g_ref, gn_ref, o_ref,
                      dmat_ref, tab_ref, sf_ref, sbrun_ref, sb_ref, *, n_tiles):
    b, ph, i = pl.program_id(0), pl.program_id(1), pl.program_id(2)
    C, H, DK, DV = RET_CHUNK, RET_HEADS, RET_QK_DIM, RET_V_DIM

    @pl.when(jnp.logical_and(b == 0, jnp.logical_and(ph == 0, i == 0)))
    def _():
        row = lax.broadcasted_iota(I32, (C, C), 0).astype(F32)
        col = lax.broadcasted_iota(I32, (C, C), 1).astype(F32)
        diff = row - col
        for h in range(H):
            fwd = jnp.exp(jnp.maximum(diff, 0.0) * lg_ref[0, h])
            bwd = jnp.exp(jnp.maximum(-diff, 0.0) * lg_ref[1, h])
            dmat_ref[h] = jnp.where(diff >= 0, fwd, bwd)
        pos = lax.broadcasted_iota(I32, (C, RET_QK_W), 0).astype(F32)
        head = lax.broadcasted_iota(I32, (C, RET_QK_W), 1) >> 6
        lgf = jnp.zeros((C, RET_QK_W), F32)
        lgb = jnp.zeros((C, RET_QK_W), F32)
        for h in range(H):
            lgf = jnp.where(head == h, lg_ref[0, h], lgf)
            lgb = jnp.where(head == h, lg_ref[1, h], lgb)
        tab_ref[0] = jnp.exp((C - 1.0 - pos) * lgf)
        tab_ref[1] = jnp.exp((pos + 1.0) * lgf)
        tab_ref[2] = jnp.exp(pos * lgb)
        tab_ref[3] = jnp.exp((C - pos) * lgb)
        tab_ref[4] = jnp.exp(C * lgf)
        tab_ref[5] = jnp.exp(C * lgb)

    def chunk_decay(idx):
        return tab_ref[idx][:DV, :].T

    def kv_state(kd, v):
        kdt = kd.T.astype(BF16)
        parts = [jnp.dot(kdt[h * DK:(h + 1) * DK, :], v[:, h * DV:(h + 1) * DV], preferred_element_type=F32)
                 for h in range(H)]
        return jnp.concatenate(parts, axis=0)

    @pl.when(ph == 0)
    def _():
        @pl.when(i == 0)
        def _():
            sbrun_ref[...] = jnp.zeros_like(sbrun_ref)

        tile = n_tiles - 1 - i
        dec = chunk_decay(5)
        for c in reversed(range(RET_CPT)):
            rows = pl.ds(c * C, C)
            sb_ref[tile * RET_CPT + c] = sbrun_ref[...]
            kd = k_ref[rows, :].astype(F32) * tab_ref[2]
            sbrun_ref[...] = dec * sbrun_ref[...] + kv_state(kd, v_ref[rows, :])

    @pl.when(ph == 1)
    def _():
        @pl.when(i == 0)
        def _():
            sf_ref[...] = jnp.zeros_like(sf_ref)

        dec = chunk_decay(4)
        head = lax.broadcasted_iota(I32, (C, RET_QK_W), 1) >> 6
        for c in range(RET_CPT):
            rows = pl.ds(c * C, C)
            q = q_ref[rows, :].astype(F32)
            k = k_ref[rows, :].astype(F32)
            v = v_ref[rows, :]
            kt = k.T.astype(BF16)
            qf = q * tab_ref[1]
            qb = q * tab_ref[3]
            sf = sf_ref[...].astype(BF16)
            sb = sb_ref[i * RET_CPT + c].astype(BF16)
            outs = []
            for h in range(H):
                sel = head == h
                qh = jnp.where(sel, q, 0.0).astype(BF16)
                s = jnp.dot(qh, kt, preferred_element_type=F32) * dmat_ref[h]
                o = jnp.dot(s.astype(BF16), v[:, h * DV:(h + 1) * DV], preferred_element_type=F32)
                o += jnp.dot(jnp.where(sel, qf, 0.0).astype(BF16), sf, preferred_element_type=F32)
                o += jnp.dot(jnp.where(sel, qb, 0.0).astype(BF16), sb, preferred_element_type=F32)
                mu = jnp.mean(o, axis=-1, keepdims=True)
                var = jnp.mean(jnp.square(o - mu), axis=-1, keepdims=True)
                outs.append((o - mu) * lax.rsqrt(var + NORM_EPS))
            y = jnp.concatenate(outs, axis=1) * gn_ref[...]
            o_ref[rows, :] = (y * g_ref[rows, :].astype(F32)).astype(BF16)
            sf_ref[...] = dec * sf_ref[...] + kv_state(k * tab_ref[0], v)


def _retention(proj, lg, gn_gain, B, S):
    nt = S // RET_TILE
    nc = S // RET_CHUNK
    qkw = RET_QK_W

    def kv_idx(blk):
        def f(b, ph, i, lg_ref):
            t = jnp.where(ph == 0, nt - 1 - i, i)
            return (b * nt + t, blk)
        return f

    def out_idx(blk):
        def f(b, ph, i, lg_ref):
            return (b * nt + i * ph, blk)
        return f

    grid_spec = pltpu.PrefetchScalarGridSpec(
        num_scalar_prefetch=1,
        grid=(B, 2, nt),
        in_specs=[
            pl.BlockSpec((RET_TILE, qkw), out_idx(2 * CB_RQK)),
            pl.BlockSpec((RET_TILE, qkw), kv_idx(2 * CB_RQK + 1)),
            pl.BlockSpec((RET_TILE, RET_V_W), kv_idx(CB_RV)),
            pl.BlockSpec((RET_TILE, RET_V_W), out_idx(CB_RG)),
            pl.BlockSpec((1, RET_V_W), lambda b, ph, i, lg_ref: (0, 0)),
        ],
        out_specs=pl.BlockSpec((RET_TILE, RET_V_W), out_idx(0)),
        scratch_shapes=[
            pltpu.VMEM((RET_HEADS, RET_CHUNK, RET_CHUNK), F32),
            pltpu.VMEM((6, RET_CHUNK, qkw), F32),
            pltpu.VMEM((qkw, RET_V_DIM), F32),
            pltpu.VMEM((qkw, RET_V_DIM), F32),
            pltpu.VMEM((nc, qkw, RET_V_DIM), F32),
        ],
    )
    return pl.pallas_call(
        functools.partial(_retention_kernel, n_tiles=nt),
        grid_spec=grid_spec,
        out_shape=jax.ShapeDtypeStruct((B * S, RET_V_W), BF16),
        compiler_params=_cparams(("arbitrary", "arbitrary", "arbitrary")),
        name="retention",
    )(lg, proj, proj, proj, proj, gn_gain.reshape(1, RET_V_W))


ATT_QB = 128
LOG2E = 1.4426950408889634


def _dil_attn_kernel(q_ref, km_ref, kp_ref, kn_ref, vm_ref, vp_ref, vn_ref, o_ref, lse_ref, *, L, Lq, d):
    i = pl.program_id(1)
    R, QB, HD = ATT_R, ATT_QB, ATT_HEAD_DIM
    nq = Lq // QB
    NK = QB + 2 * R
    row = lax.broadcasted_iota(I32, (QB, NK), 0)
    col = lax.broadcasted_iota(I32, (QB, NK), 1)
    band = jnp.logical_and(col - row >= 0, col - row <= 2 * R)
    lane = lax.broadcasted_iota(I32, (QB, LANES), 1)
    scale = HD ** -0.5

    def rows(ref, r, n):
        return ref[pl.ds(r, n, stride=d), :] if d > 1 else ref[...]

    def residue(r):
        q_all = rows(q_ref, r, Lq)
        k_all = jnp.concatenate([rows(kp_ref, r, R), rows(km_ref, r, Lq), rows(kn_ref, r, R)], axis=0)
        v_all = jnp.concatenate([rows(vp_ref, r, R), rows(vm_ref, r, Lq), rows(vn_ref, r, R)], axis=0)
        for qb in range(nq):
            base = i * Lq + qb * QB - R
            ok = jnp.logical_and(band, jnp.logical_and(col >= -base, col < L - base))
            q2 = _unpack_halves(q_all[qb * QB:(qb + 1) * QB])
            k2 = _unpack_halves(k_all[qb * QB:qb * QB + NK])
            v2 = _unpack_halves(v_all[qb * QB:qb * QB + NK])
            outs, lses = [], []
            for half in range(2):
                s = lax.dot_general(q2[half].astype(BF16), k2[half].astype(BF16), (((1,), (1,)), ((), ())),
                                    preferred_element_type=F32)
                s = jnp.where(ok, s, NEG_BIG)
                m = jnp.max(s, axis=-1, keepdims=True)
                p = jnp.exp2((s - m) * (scale * LOG2E))
                den = jnp.sum(p, axis=-1, keepdims=True)
                pn = (p * (1.0 / den)).astype(BF16)
                outs.append(jnp.dot(pn, v2[half].astype(BF16), preferred_element_type=F32))
                lses.append(m * scale + jnp.log(den))
            packed = _pack_halves(jnp.concatenate(outs, axis=1))
            lse_tile = jnp.where(lane < LANES // 2, lses[0], lses[1])
            dst = pl.ds(r + qb * QB * d, QB, stride=d) if d > 1 else pl.ds(qb * QB, QB)
            o_ref[dst, :] = packed
            lse_ref[dst, :] = lse_tile

    if d > 1:
        def body(r, carry):
            residue(r)
            return carry

        lax.fori_loop(0, d, body, 0, unroll=4)
    else:
        residue(0)


def _dil_attn(attp, g, B, S):
    window, d = ATT_PATTERNS[g]
    L = S // d
    Lq = min(L, 1024 if d == 1 else 256)
    tb = Lq * d
    nb = S // tb
    halo = ATT_R * d
    hb = tb // halo
    nhb = S // halo
    npair = ATTP_BLK // LANES
    cq, ck, cv = (npair * (3 * g + kind) for kind in range(3))

    def main(c):
        return pl.BlockSpec((tb, LANES), lambda b, i, p: (b * nb + i, c + p))

    def prev(c):
        return pl.BlockSpec((halo, LANES), lambda b, i, p: (b * nhb + jnp.maximum(i * hb - 1, 0), c + p))

    def nxt(c):
        return pl.BlockSpec((halo, LANES), lambda b, i, p: (b * nhb + jnp.minimum((i + 1) * hb, nhb - 1), c + p))

    out_spec = pl.BlockSpec((tb, LANES), lambda b, i, p: (b * nb + i, p))
    return pl.pallas_call(
        functools.partial(_dil_attn_kernel, L=L, Lq=Lq, d=d),
        grid=(B, nb, npair),
        in_specs=[main(cq), main(ck), prev(ck), nxt(ck), main(cv), prev(cv), nxt(cv)],
        out_specs=[out_spec, out_spec],
        out_shape=[
            jax.ShapeDtypeStruct((B * S, npair * LANES), U32),
            jax.ShapeDtypeStruct((B * S, npair * LANES), F32),
        ],
        compiler_params=_cparams(("arbitrary", "arbitrary", "arbitrary")),
        name=f"dil_attn_{g}",
    )(attp, attp, attp, attp, attp, attp, attp)


MERGE_TILE = 512


def _merge_kernel(x_ref, r_ref, o0_ref, o1_ref, o2_ref, l0_ref, l1_ref, l2_ref, su_ref, sv_ref,
                  g0_ref, g1_ref, g2_ref, wr_ref, wa_ref, ws_ref, wo_ref, sgw_ref, sgb_ref, lng_ref,
                  nf_ref, wrt_ref, brt_ref, tri_ref,
                  x2_ref, hp_ref, route_ref, cnt_ref, carry_ref):
    i = pl.program_id(0)
    tm = x_ref.shape[0]
    HD = ATT_HEAD_DIM

    @pl.when(i == 0)
    def _():
        carry_ref[...] = jnp.zeros_like(carry_ref)

    l0, l1, l2 = l0_ref[...], l1_ref[...], l2_ref[...]
    m = jnp.maximum(jnp.maximum(l0, l1), l2)
    e0, e1, e2 = jnp.exp(l0 - m), jnp.exp(l1 - m), jnp.exp(l2 - m)
    inv = 1.0 / (e0 + e1 + e2)
    wts = (e0 * inv, e1 * inv, e2 * inv)
    outs = [jnp.concatenate(_unpack_halves(o_ref[...]), axis=1) for o_ref in (o0_ref, o1_ref, o2_ref)]
    heads = []
    for h in range(ATT_HEADS):
        cols = slice(h * HD, (h + 1) * HD)
        c = (h % 2) * LANES + (h // 2) * (LANES // 2)
        heads.append(sum(w[:, c:c + 1] * o[:, cols] for w, o in zip(wts, outs)))
    att = jnp.concatenate(heads, axis=1).astype(BF16)
    y_att = jnp.dot(att, wa_ref[...], preferred_element_type=F32)
    y_ret = jnp.dot(r_ref[...], wr_ref[...], preferred_element_type=F32)

    u = su_ref[...].astype(F32)
    v = sv_ref[...].astype(F32)
    mu = jnp.mean(v, axis=-1, keepdims=True)
    var = jnp.mean(jnp.square(v - mu), axis=-1, keepdims=True)
    vn = ((v - mu) * lax.rsqrt(var + NORM_EPS) * lng_ref[...]).astype(BF16)
    gw = SG_WIDTH // SG_GROUPS
    chunks = []
    for c in range(tm // SG_CHUNK):
        rows = slice(c * SG_CHUNK, (c + 1) * SG_CHUNK)
        parts = [jnp.dot(sgw_ref[g], vn[rows, g * gw:(g + 1) * gw], preferred_element_type=F32)
                 for g in range(SG_GROUPS)]
        chunks.append(jnp.concatenate(parts, axis=1) + sgb_ref[...])
    vs = jnp.concatenate(chunks, axis=0)
    y_sg = jnp.dot((u * vs).astype(BF16), ws_ref[...], preferred_element_type=F32)

    merged = (g0_ref[...].astype(F32) * y_ret + g1_ref[...].astype(F32) * y_att
              + g2_ref[...].astype(F32) * y_sg)
    x2 = x_ref[...] + jnp.dot(merged.astype(BF16), wo_ref[...], preferred_element_type=F32)
    x2_ref[...] = x2

    ms = jnp.mean(x2 * x2, axis=-1, keepdims=True)
    h2 = x2 * lax.rsqrt(ms + NORM_EPS) * nf_ref[...]
    hp_ref[...] = _pack_halves(h2)

    h_hi = h2.astype(BF16)
    h_lo = (h2 - h_hi.astype(F32)).astype(BF16)
    hw = jnp.dot(h_hi, wrt_ref[...], preferred_element_type=F32)
    logits = (hw[:, :LANES] + hw[:, LANES:] + jnp.dot(h_lo, wrt_ref[:, :LANES], preferred_element_type=F32)
              + brt_ref[...])
    lane = lax.broadcasted_iota(I32, logits.shape, 1)
    big = jnp.int32(LANES)
    is_g = lane < N_GROUPS
    gl = jnp.where(is_g, logits, -jnp.inf)
    gmax = jnp.max(gl, axis=-1, keepdims=True)
    g_idx = jnp.min(jnp.where(jnp.logical_and(is_g, gl == gmax), lane, big), axis=-1, keepdims=True)
    g_w = 1.0 / jnp.sum(jnp.where(is_g, jnp.exp(gl - gmax), 0.0), axis=-1, keepdims=True)
    lo = ROUTE_LANE0 + g_idx * EXP_PER_GROUP
    in_grp = jnp.logical_and(lane >= lo, lane < lo + EXP_PER_GROUP)
    el = jnp.where(in_grp, logits, -jnp.inf)
    v1 = jnp.max(el, axis=-1, keepdims=True)
    i1 = jnp.min(jnp.where(el == v1, lane, big), axis=-1, keepdims=True)
    el2 = jnp.where(lane == i1, -jnp.inf, el)
    v2 = jnp.max(el2, axis=-1, keepdims=True)
    i2 = jnp.min(jnp.where(el2 == v2, lane, big), axis=-1, keepdims=True)
    t = jnp.exp(v2 - v1)
    w1 = g_w / (1.0 + t)
    w2 = g_w * t / (1.0 + t)

    oh1 = lane == i1
    oh2 = lane == i2
    both = jnp.where(jnp.logical_or(oh1, oh2), 1.0, 0.0)
    before = jnp.dot(tri_ref[...], both.astype(BF16), preferred_element_type=F32) + carry_ref[0:1, :]
    rank1 = jnp.sum(jnp.where(oh1, before, 0.0), axis=-1, keepdims=True)
    rank2 = jnp.sum(jnp.where(oh2, before, 0.0), axis=-1, keepdims=True)
    carry = carry_ref[0:1, :] + jnp.sum(both, axis=0, keepdims=True)
    carry_ref[...] = jnp.broadcast_to(carry, carry_ref.shape)
    cnt_ref[...] = jnp.broadcast_to(carry, cnt_ref.shape)

    out = jnp.where(lane == 0, (i1 - ROUTE_LANE0).astype(F32), 0.0)
    out = jnp.where(lane == 1, (i2 - ROUTE_LANE0).astype(F32), out)
    out = jnp.where(lane == 2, w1, out)
    out = jnp.where(lane == 3, w2, out)
    out = jnp.where(lane == 4, rank1, out)
    out = jnp.where(lane == 5, rank2, out)
    route_ref[...] = out


def _merge(x2d, r, os_, lses, proj, lw):
    T = x2d.shape[0]
    tm = MERGE_TILE
    row = lambda w: pl.BlockSpec((tm, w), lambda i: (i, 0))
    pcol = lambda w, cb: pl.BlockSpec((tm, w), lambda i: (i, cb))
    full = lambda a: pl.BlockSpec(a.shape, lambda i: (0,) * a.ndim)
    gate_cb = CB_GATE * COL_BLK // D_MODEL
    weights = [lw["w_ret_out"], lw["w_att_out"], lw["w_sg_out"], lw["w_o"], lw["sg_w"], lw["sg_b_full"],
               lw["sg_ln_gain"], lw["norm_ffn"], lw["w_router"], lw["b_router"], lw["tri"]]
    return pl.pallas_call(
        _merge_kernel,
        grid=(T // tm,),
        in_specs=[row(D_MODEL), row(RET_V_W)] + [row(2 * LANES)] * (2 * len(ATT_PATTERNS)) + [
                  pcol(SG_WIDTH, CB_SGU), pcol(SG_WIDTH, CB_SGV),
                  pcol(D_MODEL, gate_cb), pcol(D_MODEL, gate_cb + 1), pcol(D_MODEL, gate_cb + 2)
                 ] + [full(a) for a in weights],
        out_specs=[row(D_MODEL), row(D_MODEL // 2), row(LANES), pl.BlockSpec((8, LANES), lambda i: (0, 0))],
        out_shape=[jax.ShapeDtypeStruct((T, D_MODEL), F32), jax.ShapeDtypeStruct((T, D_MODEL // 2), U32),
                   jax.ShapeDtypeStruct((T, LANES), F32), jax.ShapeDtypeStruct((8, LANES), F32)],
        scratch_shapes=[pltpu.VMEM((8, LANES), F32)],
        compiler_params=_cparams(("arbitrary",)),
        name="merge",
    )(x2d, r, *os_, *lses, proj, proj, proj, proj, proj, *weights)


DISPATCH_TILE = 1024
ISSUE_UNROLL = 8


def _dispatch_kernel(dest_ref, h_ref, xb_in_ref, xb_ref, sem):
    del xb_in_ref
    tm = DISPATCH_TILE

    def issue(g, carry):
        base = pl.multiple_of(g * ISSUE_UNROLL, ISSUE_UNROLL)
        for u in range(ISSUE_UNROLL):
            src = h_ref.at[pl.ds(base + u, 1)]
            for k in range(TOP_K):
                pltpu.make_async_copy(src, xb_ref.at[pl.ds(dest_ref[0, 0, TOP_K * (base + u) + k], 1)],
                                      sem).start(priority=k)
        return carry

    lax.fori_loop(0, tm // ISSUE_UNROLL, issue, 0)
    pltpu.make_async_copy(xb_ref.at[pl.ds(0, TOP_K * tm)], xb_ref.at[pl.ds(0, TOP_K * tm)], sem).wait()


def _dispatch(hp, dest, P):
    T = hp.shape[0]
    tm = DISPATCH_TILE
    half = D_MODEL // 2
    return pl.pallas_call(
        _dispatch_kernel,
        grid=(T // tm,),
        in_specs=[
            pl.BlockSpec((1, 1, TOP_K * tm), lambda i: (i, 0, 0), memory_space=pltpu.SMEM),
            pl.BlockSpec((tm, half), lambda i: (i, 0)),
            pl.BlockSpec(memory_space=pl.ANY),
        ],
        out_specs=pl.BlockSpec(memory_space=pl.ANY),
        out_shape=jax.ShapeDtypeStruct((P, half), U32),
        scratch_shapes=[pltpu.SemaphoreType.DMA(())],
        input_output_aliases={2: 0},
        compiler_params=_cparams(("arbitrary",)),
        name="dispatch",
    )(dest.reshape(T // tm, 1, TOP_K * tm), hp, jnp.zeros((P, half), U32))


def _experts_kernel(blk_e_ref, nused_ref, x_ref, wg_ref, wu_ref, wd_ref, y_ref, wg_s, wu_s, wd_s):
    i = pl.program_id(0)

    @pl.when(jnp.logical_or(i == 0, blk_e_ref[i] != blk_e_ref[jnp.maximum(i - 1, 0)]))
    def _():
        wg_s[...] = wg_ref[0, 0].astype(BF16)
        wu_s[...] = wu_ref[0, 0].astype(BF16)
        wd_s[...] = wd_ref[0, 0].astype(BF16)

    @pl.when(i < nused_ref[0])
    def _():
        x = jnp.concatenate(_unpack_halves(x_ref[...]), axis=1).astype(BF16)
        gate = jnp.dot(x, wg_s[...], preferred_element_type=F32)
        up = jnp.dot(x, wu_s[...], preferred_element_type=F32)
        hid = (gate * _sigmoid(gate) * up).astype(BF16)
        y_ref[...] = jnp.dot(hid, wd_s[...], preferred_element_type=F32)

    @pl.when(i >= nused_ref[0])
    def _():
        y_ref[...] = jnp.zeros_like(y_ref)


def _experts(xb, blk_e, nused, wg, wu, wd, layer):
    P = xb.shape[0]
    bm = MOE_BLOCK
    grid_spec = pltpu.PrefetchScalarGridSpec(
        num_scalar_prefetch=2,
        grid=(P // bm,),
        in_specs=[
            pl.BlockSpec((bm, D_MODEL // 2), lambda i, e, n: (i, 0)),
            pl.BlockSpec((1, 1, D_MODEL, EXP_HIDDEN), lambda i, e, n: (layer, e[i], 0, 0)),
            pl.BlockSpec((1, 1, D_MODEL, EXP_HIDDEN), lambda i, e, n: (layer, e[i], 0, 0)),
            pl.BlockSpec((1, 1, EXP_HIDDEN, D_MODEL), lambda i, e, n: (layer, e[i], 0, 0)),
        ],
        out_specs=pl.BlockSpec((bm, D_MODEL), lambda i, e, n: (i, 0)),
        scratch_shapes=[pltpu.VMEM((D_MODEL, EXP_HIDDEN), BF16), pltpu.VMEM((D_MODEL, EXP_HIDDEN), BF16),
                        pltpu.VMEM((EXP_HIDDEN, D_MODEL), BF16)],
    )
    return pl.pallas_call(
        _experts_kernel,
        grid_spec=grid_spec,
        out_shape=jax.ShapeDtypeStruct((P, D_MODEL), F32),
        compiler_params=_cparams(("arbitrary",)),
        name="experts",
    )(blk_e, nused, xb, wg, wu, wd)


COMBINE_TILE = 256


def _combine_kernel(dest_ref, dest_next_ref, x_ref, route_ref, gain_ref, y_ref, o_ref, buf_ref, sems, *, final_norm):
    i = pl.program_id(0)
    n = pl.num_programs(0)
    tm = COMBINE_TILE

    def gather(idx_ref, slot):
        def issue(g, carry):
            base = pl.multiple_of(g * ISSUE_UNROLL, ISSUE_UNROLL)
            for u in range(ISSUE_UNROLL):
                for k in range(TOP_K):
                    pltpu.make_async_copy(y_ref.at[pl.ds(idx_ref[0, 0, TOP_K * (base + u) + k], 1)],
                                          buf_ref.at[slot, k, pl.ds(base + u, 1)], sems.at[slot]).start(priority=k)
            return carry

        lax.fori_loop(0, tm // ISSUE_UNROLL, issue, 0)

    @pl.when(i == 0)
    def _():
        gather(dest_ref, 0)

    def step(slot):
        @pl.when(i + 1 < n)
        def _():
            gather(dest_next_ref, 1 - slot)

        pltpu.make_async_copy(buf_ref.at[slot], buf_ref.at[slot], sems.at[slot]).wait()
        route = route_ref[...]
        x = x_ref[...] + route[:, 2:3] * buf_ref[slot, 0] + route[:, 3:4] * buf_ref[slot, 1]
        if final_norm:
            ms = jnp.mean(x * x, axis=-1, keepdims=True)
            x = x * lax.rsqrt(ms + NORM_EPS) * gain_ref[...]
        o_ref[...] = x

    for slot in range(2):
        pl.when(i % 2 == slot)(functools.partial(step, slot))


def _combine(x2, route, yb, dest, gain, final_norm):
    T = x2.shape[0]
    tm = COMBINE_TILE
    nt = T // tm
    dest3 = dest.reshape(nt, 1, TOP_K * tm)
    return pl.pallas_call(
        functools.partial(_combine_kernel, final_norm=final_norm),
        grid=(nt,),
        in_specs=[
            pl.BlockSpec((1, 1, TOP_K * tm), lambda i: (i, 0, 0), memory_space=pltpu.SMEM),
            pl.BlockSpec((1, 1, TOP_K * tm), lambda i: (jnp.minimum(i + 1, nt - 1), 0, 0), memory_space=pltpu.SMEM),
            pl.BlockSpec((tm, D_MODEL), lambda i: (i, 0)),
            pl.BlockSpec((tm, LANES), lambda i: (i, 0)),
            pl.BlockSpec((1, D_MODEL), lambda i: (0, 0)),
            pl.BlockSpec(memory_space=pl.ANY),
        ],
        out_specs=pl.BlockSpec((tm, D_MODEL), lambda i: (i, 0)),
        out_shape=jax.ShapeDtypeStruct((T, D_MODEL), F32),
        scratch_shapes=[pltpu.VMEM((2, TOP_K, tm, D_MODEL), F32), pltpu.SemaphoreType.DMA((2,))],
        compiler_params=_cparams(("arbitrary",)),
        name="combine_final" if final_norm else "combine",
    )(dest3, dest3, x2, route, gain.reshape(1, D_MODEL), yb)


def _layer_weights(l, p):
    w_router = jnp.zeros((D_MODEL, LANES), F32)
    w_router = w_router.at[:, :N_GROUPS].set(p["w_router_group"][l])
    w_router = w_router.at[:, ROUTE_LANE0:ROUTE_LANE0 + N_EXPERTS].set(p["w_router_expert"][l])
    b_router = jnp.zeros((1, LANES), F32)
    b_router = b_router.at[0, :N_GROUPS].set(p["b_router_group"][l])
    b_router = b_router.at[0, ROUTE_LANE0:ROUTE_LANE0 + N_EXPERTS].set(p["b_router_expert"][l])
    tm = MERGE_TILE
    tri = (jnp.arange(tm)[:, None] > jnp.arange(tm)[None, :]).astype(BF16)
    gw = SG_WIDTH // SG_GROUPS
    w_in, w_in_rot = _in_proj_weights(p["w_in"][l])
    return {
        "norm_mix": p["norm_mix"][l],
        "w_in": w_in,
        "w_in_rot": w_in_rot,
        "lg": jnp.stack([-jnp.exp(p["ret_decay_fwd"][l].astype(F32)), -jnp.exp(p["ret_decay_bwd"][l].astype(F32))]),
        "ret_gn_gain": p["ret_gn_gain"][l],
        "w_ret_out": p["w_ret_out"][l].astype(BF16),
        "w_att_out": p["w_att_out"][l].astype(BF16),
        "w_sg_out": p["w_sg_out"][l].astype(BF16),
        "w_o": p["w_o"][l].astype(BF16),
        "sg_w": p["sg_w"][l].astype(BF16),
        "sg_b_full": jnp.repeat(p["sg_b"][l].T, gw, axis=1),
        "sg_ln_gain": p["sg_ln_gain"][l].reshape(1, SG_WIDTH),
        "norm_ffn": p["norm_ffn"][l].reshape(1, D_MODEL),
        "w_router": jnp.concatenate([w_router.astype(BF16),
                                     (w_router - w_router.astype(BF16).astype(F32)).astype(BF16)], axis=1),
        "b_router": b_router,
        "tri": tri,
    }


def _moe_plan(route, cnt, T):
    bm = MOE_BLOCK
    eid = route[:, 0:TOP_K].astype(I32)
    rank = route[:, 4:4 + TOP_K].astype(I32)
    counts = cnt[0, ROUTE_LANE0:ROUTE_LANE0 + N_EXPERTS].astype(I32)
    padded = (counts + bm - 1) // bm * bm
    pad_end = jnp.cumsum(padded)
    pad_start = pad_end - padded
    experts = jnp.arange(N_EXPERTS, dtype=I32)
    dest = jnp.sum(jnp.where(eid[..., None] == experts, pad_start, 0), axis=-1) + rank
    P = T * TOP_K + N_EXPERTS * bm
    nblk = P // bm
    blk_start = jnp.arange(nblk, dtype=I32) * bm
    blk_e = jnp.minimum(jnp.sum((pad_end[None, :] <= blk_start[:, None]).astype(I32), axis=1), N_EXPERTS - 1)
    nused = (pad_end[-1:] // bm).astype(I32)
    return dest.reshape(-1), blk_e, nused, P


def _run_trunk(x, lws, w_exp, tabs, norm_final):
    B, S, _ = x.shape
    T = B * S
    x2d = x.reshape(T, D_MODEL)
    tm_in = min(2048, S)
    for l, lw in enumerate(lws):
        proj, attp = _in_proj(x2d, lw["norm_mix"], lw["w_in"], lw["w_in_rot"], tabs, S, tm_in)
        r = _retention(proj, lw["lg"], lw["ret_gn_gain"], B, S)
        os_, lses = zip(*[_dil_attn(attp, g, B, S) for g in range(len(ATT_PATTERNS))])
        x2, hp, route, cnt = _merge(x2d, r, os_, lses, proj, lw)
        dest, blk_e, nused, P = _moe_plan(route, cnt, T)
        xb = _dispatch(hp, dest, P)
        yb = _experts(xb, blk_e, nused, *w_exp, layer=l)
        x2d = _combine(x2, route, yb, dest, norm_final, final_norm=(l == len(lws) - 1))
    return x2d.reshape(B, S, D_MODEL)


def kernel(x_prompt, x_sample, norm_mix, w_in, ret_decay_fwd, ret_decay_bwd, ret_gn_gain, sg_ln_gain, sg_w, sg_b,
           w_ret_out, w_att_out, w_sg_out, w_o, norm_ffn, w_router_group, b_router_group, w_router_expert,
           b_router_expert, w_exp_gate, w_exp_up, w_exp_down, norm_final):
    p = dict(norm_mix=norm_mix, w_in=w_in, ret_decay_fwd=ret_decay_fwd, ret_decay_bwd=ret_decay_bwd,
             ret_gn_gain=ret_gn_gain, sg_ln_gain=sg_ln_gain, sg_w=sg_w, sg_b=sg_b, w_ret_out=w_ret_out,
             w_att_out=w_att_out, w_sg_out=w_sg_out, w_o=w_o, norm_ffn=norm_ffn, w_router_group=w_router_group,
             b_router_group=b_router_group, w_router_expert=w_router_expert, b_router_expert=b_router_expert,
             w_exp_gate=w_exp_gate, w_exp_up=w_exp_up, w_exp_down=w_exp_down)
    lws = [_layer_weights(l, p) for l in range(DEPTH)]
    s_max = max(x_prompt.shape[1], x_sample.shape[1])
    tabs = _ret_rot_tables(s_max) + _att_rot_tables(s_max)
    w_exp = (w_exp_gate, w_exp_up, w_exp_down)
    y_prompt = _run_trunk(x_prompt, lws, w_exp, tabs, norm_final)
    y_sample = _run_trunk(x_sample, lws, w_exp, tabs, norm_final)
    return (y_prompt, y_sample)
```

```python
import functools

import jax
import jax.numpy as jnp
import numpy as np
from jax import lax
from jax.experimental import pallas as pl
from jax.experimental.pallas import tpu as pltpu

F32 = jnp.float32
BF16 = jnp.bfloat16
U32 = jnp.uint32
I32 = jnp.int32

D_MODEL = 1024
DEPTH = 2
NORM_EPS = 1e-6
RET_HEADS = 4
RET_QK_DIM = 64
RET_V_DIM = 128
RET_CHUNK = 128
RET_THETA = 10000.0
ATT_PATTERNS = ((128, 1), (512, 4), (2048, 16))
ATT_HEADS = 4
ATT_HEAD_DIM = 128
ATT_ROT_DIM = ATT_HEAD_DIM // 4
ROPE_THETA = 500000.0
SG_GROUPS = 4
SG_CHUNK = 128
SG_WIDTH = 512
N_BRANCH = 3
N_GROUPS = 4
EXP_PER_GROUP = 8
N_EXPERTS = N_GROUPS * EXP_PER_GROUP
TOP_K = 2
EXP_HIDDEN = 512
RET_QK_W = RET_HEADS * RET_QK_DIM
RET_V_W = RET_HEADS * RET_V_DIM
ATT_W = ATT_HEADS * ATT_HEAD_DIM
ATT_IN_W = len(ATT_PATTERNS) * 3 * ATT_W
SG_IN_W = 2 * SG_WIDTH
GATE_IN_W = N_BRANCH * D_MODEL
N_IN = 2 * RET_QK_W + 2 * RET_V_W + ATT_IN_W + SG_IN_W + GATE_IN_W
NEG_BIG = -1e30

LANES = 128
COL_BLK = 512
N_COL_BLK = N_IN // COL_BLK
ATT_R = 64
assert all(w // (2 * d) == ATT_R for w, d in ATT_PATTERNS)
CB_GATE, CB_RV, CB_RG, CB_SGU, CB_SGV, CB_RQK, CB_ATT = 0, 6, 7, 8, 9, 10, 11
N_PROJ_BLK = CB_ATT
N_ATT_BLK = N_COL_BLK - CB_ATT
ATT_PAIR_SHIFT = LANES // 2
ATTP_BLK = COL_BLK // 2
ROUTE_LANE0 = N_GROUPS
RT_EID, RT_W, RT_RANK = 0, TOP_K, 2 * TOP_K
MOE_BLOCK = 512
VMEM_LIMIT = 56 * 1024 * 1024
HI_HALF = 0xFFFF0000
IN_PROJ_ROWS = 256


def _cparams(sem, vmem=VMEM_LIMIT):
    return pltpu.CompilerParams(dimension_semantics=sem, vmem_limit_bytes=vmem)


def _sigmoid(x):
    return 0.5 * jnp.tanh(0.5 * x) + 0.5


def _pack_halves(y):
    n = y.shape[1] // 2
    bits = lax.bitcast_convert_type(y.astype(BF16).astype(F32), U32)
    return (bits[:, :n] >> 16) | (bits[:, n:] & jnp.uint32(HI_HALF))


def _unpack_halves(p):
    return (lax.bitcast_convert_type(p << 16, F32), lax.bitcast_convert_type(p & jnp.uint32(HI_HALF), F32))


def _in_proj_kernel(x_ref, g_ref, w_ref, wrot_ref, rc_ref, rs_ref, ac_ref, as_ref, o_ref, oa_ref, h_ref):
    j = pl.program_id(1)

    chunks = [pl.ds(r * IN_PROJ_ROWS, IN_PROJ_ROWS) for r in range(x_ref.shape[0] // IN_PROJ_ROWS)]

    @pl.when(j == 0)
    def _():
        for rows in chunks:
            x = x_ref[rows, :]
            ms = jnp.mean(x * x, axis=-1, keepdims=True)
            h = (x * lax.rsqrt(ms + NORM_EPS) * g_ref[...]).astype(BF16)
            h_ref[rows, :] = h
            o_ref[rows, :] = _sigmoid(jnp.dot(h, w_ref[...], preferred_element_type=F32)).astype(BF16)

    def dot(w, rows):
        return jnp.dot(h_ref[rows, :], w[...], preferred_element_type=F32)

    reps = COL_BLK // LANES

    def tiled(tab_ref, rows):
        return jnp.concatenate([tab_ref[rows, :]] * reps, axis=1)

    att_kind = (j - CB_ATT) % 3

    @pl.when(jnp.logical_and(j > 0, j < CB_RV))
    def _():
        for rows in chunks:
            o_ref[rows, :] = _sigmoid(dot(w_ref, rows)).astype(BF16)

    @pl.when(j == CB_RV)
    def _():
        for rows in chunks:
            o_ref[rows, :] = dot(w_ref, rows).astype(BF16)

    @pl.when(j == CB_RG)
    def _():
        for rows in chunks:
            acc = dot(w_ref, rows)
            o_ref[rows, :] = (acc * _sigmoid(acc)).astype(BF16)

    @pl.when(jnp.logical_or(j == CB_SGU, j == CB_SGV))
    def _():
        for rows in chunks:
            o_ref[rows, :] = jax.nn.gelu(dot(w_ref, rows)).astype(BF16)

    @pl.when(j == CB_RQK)
    def _():
        for rows in chunks:
            y = dot(w_ref, rows) * tiled(rc_ref, rows) + dot(wrot_ref, rows) * tiled(rs_ref, rows)
            lane = lax.broadcasted_iota(I32, y.shape, 1)
            y = jnp.where(lane >= RET_QK_W, y * (RET_QK_DIM ** -0.5), y)
            o_ref[rows, :] = y.astype(BF16)

    @pl.when(jnp.logical_and(j >= CB_ATT, att_kind != 2))
    def _():
        for rows in chunks:
            acc = dot(w_ref, rows)
            partner = jnp.concatenate([pltpu.roll(acc[:, c * LANES:(c + 1) * LANES], ATT_PAIR_SHIFT, 1)
                                       for c in range(reps)], axis=1)
            oa_ref[rows, :] = _pack_halves(acc * tiled(ac_ref, rows) + partner * tiled(as_ref, rows))

    @pl.when(jnp.logical_and(j >= CB_ATT, att_kind == 2))
    def _():
        for rows in chunks:
            oa_ref[rows, :] = _pack_halves(dot(w_ref, rows))


def _rot_angles(S, n_rot, theta):
    half = n_rot // 2
    inv = jnp.power(jnp.float32(theta), -jnp.arange(half, dtype=F32) * (2.0 / n_rot))
    ang = jnp.arange(S, dtype=F32)[:, None] * inv[None, :]
    return jnp.cos(ang), jnp.sin(ang)


def _ret_rot_tables(S):
    cos, sin = _rot_angles(S, RET_QK_DIM, RET_THETA)
    return jnp.tile(cos, (1, LANES // cos.shape[1])), jnp.tile(sin, (1, LANES // sin.shape[1]))


def _att_rot_tables(S):
    cos, sin = _rot_angles(S, ATT_ROT_DIM, ROPE_THETA)
    half = ATT_ROT_DIM // 2
    gap = jnp.ones((S, ATT_PAIR_SHIFT - half), F32)
    cos_h = jnp.concatenate([cos, gap, cos, gap], axis=1)
    sin_h = jnp.concatenate([-sin, 0.0 * gap, sin, 0.0 * gap], axis=1)
    return cos_h, sin_h


def _att_head_order(w):
    half = ATT_ROT_DIM // 2
    k, n = w.shape
    wh = w.reshape(k, n // ATT_HEAD_DIM, ATT_HEAD_DIM)
    s = ATT_PAIR_SHIFT
    out = jnp.concatenate([wh[..., :half], wh[..., s:s + half], wh[..., 2 * half:s], wh[..., half:2 * half],
                           wh[..., s + half:]], axis=-1)
    return out.reshape(k, n)


def _rot_partner_cols(w, n_rot, period):
    half = n_rot // 2
    k, n = w.shape
    wh = w.reshape(k, n // period, period)
    out = jnp.concatenate([-wh[..., half:n_rot], wh[..., :half], jnp.zeros_like(wh[..., n_rot:])], axis=-1)
    return out.reshape(k, n)


def _in_proj_weights(w_in):
    o = np.cumsum([0, RET_QK_W, RET_QK_W, RET_V_W, RET_V_W, ATT_IN_W, SG_IN_W, GATE_IN_W])
    rqk, rv, rg, att, sg, gate = (w_in[:, o[0]:o[2]], w_in[:, o[2]:o[3]], w_in[:, o[3]:o[4]], w_in[:, o[4]:o[5]],
                                  w_in[:, o[5]:o[6]], w_in[:, o[6]:o[7]])
    att_blocks = []
    for blk in range(N_ATT_BLK):
        cols = att[:, blk * ATT_W:(blk + 1) * ATT_W]
        att_blocks.append(cols if blk % 3 == 2 else _att_head_order(cols))
    w = jnp.concatenate([gate, rv, rg, sg, rqk] + att_blocks, axis=1)
    return w.astype(BF16), _rot_partner_cols(rqk, RET_QK_DIM, RET_QK_DIM).astype(BF16)


def _in_proj(x2d, gain, w_bf16, wrot_bf16, tabs, S, tm):
    T = x2d.shape[0]
    nrep = S // tm
    tab_spec = pl.BlockSpec((tm, LANES), lambda i, j: (i % nrep, 0))

    return pl.pallas_call(
        _in_proj_kernel,
        grid=(T // tm, N_COL_BLK),
        in_specs=[
            pl.BlockSpec((tm, D_MODEL), lambda i, j: (i, 0)),
            pl.BlockSpec((1, D_MODEL), lambda i, j: (0, 0)),
            pl.BlockSpec((D_MODEL, COL_BLK), lambda i, j: (0, j)),
            pl.BlockSpec((D_MODEL, COL_BLK), lambda i, j: (0, 0)),
            tab_spec, tab_spec, tab_spec, tab_spec,
        ],
        out_specs=[
            pl.BlockSpec((tm, COL_BLK), lambda i, j: (i, jnp.minimum(j, N_PROJ_BLK - 1))),
            pl.BlockSpec((tm, ATTP_BLK), lambda i, j: (i, jnp.maximum(j - CB_ATT, 0))),
        ],
        out_shape=[
            jax.ShapeDtypeStruct((T, N_PROJ_BLK * COL_BLK), BF16),
            jax.ShapeDtypeStruct((T, N_ATT_BLK * ATTP_BLK), U32),
        ],
        scratch_shapes=[pltpu.VMEM((tm, D_MODEL), BF16)],
        compiler_params=_cparams(("arbitrary", "arbitrary")),
        name="in_proj",
    )(x2d, gain.reshape(1, D_MODEL), w_bf16, wrot_bf16, *tabs)


RET_TILE = 512
RET_CPT = RET_TILE // RET_CHUNK


def _retention_kernel(lg_ref, q_ref, k_ref, v_ref, g_ref, gn_ref, o_ref,
                      dmat_ref, tab_ref, sf_ref, sbrun_ref, sb_ref, *, n_tiles):
    b, ph, i = pl.program_id(0), pl.program_id(1), pl.program_id(2)
    C, H, DK, DV = RET_CHUNK, RET_HEADS, RET_QK_DIM, RET_V_DIM

    @pl.when(jnp.logical_and(b == 0, jnp.logical_and(ph == 0, i == 0)))
    def _():
        row = lax.broadcasted_iota(I32, (C, C), 0).astype(F32)
        col = lax.broadcasted_iota(I32, (C, C), 1).astype(F32)
        diff = row - col
        for h in range(H):
            fwd = jnp.exp(jnp.maximum(diff, 0.0) * lg_ref[0, h])
            bwd = jnp.exp(jnp.maximum(-diff, 0.0) * lg_ref[1, h])
            dmat_ref[h] = jnp.where(diff >= 0, fwd, bwd)
        pos = lax.broadcasted_iota(I32, (C, RET_QK_W), 0).astype(F32)
        head = lax.broadcasted_iota(I32, (C, RET_QK_W), 1) >> 6
        lgf = jnp.zeros((C, RET_QK_W), F32)
        lgb = jnp.zeros((C, RET_QK_W), F32)
        for h in range(H):
            lgf = jnp.where(head == h, lg_ref[0, h], lgf)
            lgb = jnp.where(head == h, lg_ref[1, h], lgb)
        tab_ref[0] = jnp.exp((C - 1.0 - pos) * lgf)
        tab_ref[1] = jnp.exp((pos + 1.0) * lgf)
        tab_ref[2] = jnp.exp(pos * lgb)
        tab_ref[3] = jnp.exp((C - pos) * lgb)
        tab_ref[4] = jnp.exp(C * lgf)
        tab_ref[5] = jnp.exp(C * lgb)

    def chunk_decay(idx):
        return tab_ref[idx][:DV, :].T

    def kv_state(kd, v):
        kdt = kd.T.astype(BF16)
        parts = [jnp.dot(kdt[h * DK:(h + 1) * DK, :], v[:, h * DV:(h + 1) * DV], preferred_element_type=F32)
                 for h in range(H)]
        return jnp.concatenate(parts, axis=0)

    @pl.when(ph == 0)
    def _():
        @pl.when(i == 0)
        def _():
            sbrun_ref[...] = jnp.zeros_like(sbrun_ref)

        tile = n_tiles - 1 - i
        dec = chunk_decay(5)
        for c in reversed(range(RET_CPT)):
            rows = pl.ds(c * C, C)
            sb_ref[tile * RET_CPT + c] = sbrun_ref[...]
            kd = k_ref[rows, :].astype(F32) * tab_ref[2]
            sbrun_ref[...] = dec * sbrun_ref[...] + kv_state(kd, v_ref[rows, :])

    @pl.when(ph == 1)
    def _():
        @pl.when(i == 0)
        def _():
            sf_ref[...] = jnp.zeros_like(sf_ref)

        dec = chunk_decay(4)
        head = lax.broadcasted_iota(I32, (C, RET_QK_W), 1) >> 6
        for c in range(RET_CPT):
            rows = pl.ds(c * C, C)
            q = q_ref[rows, :].astype(F32)
            k = k_ref[rows, :].astype(F32)
            v = v_ref[rows, :]
            kt = k.T.astype(BF16)
            qf = q * tab_ref[1]
            qb = q * tab_ref[3]
            sf = sf_ref[...].astype(BF16)
            sb = sb_ref[i * RET_CPT + c].astype(BF16)
            outs = []
            for h in range(H):
                sel = head == h
                qh = jnp.where(sel, q, 0.0).astype(BF16)
                s = jnp.dot(qh, kt, preferred_element_type=F32) * dmat_ref[h]
                o = jnp.dot(s.astype(BF16), v[:, h * DV:(h + 1) * DV], preferred_element_type=F32)
                o += jnp.dot(jnp.where(sel, qf, 0.0).astype(BF16), sf, preferred_element_type=F32)
                o += jnp.dot(jnp.where(sel, qb, 0.0).astype(BF16), sb, preferred_element_type=F32)
                mu = jnp.mean(o, axis=-1, keepdims=True)
                var = jnp.mean(jnp.square(o - mu), axis=-1, keepdims=True)
                outs.append((o - mu) * lax.rsqrt(var + NORM_EPS))
            y = jnp.concatenate(outs, axis=1) * gn_ref[...]
            o_ref[rows, :] = (y * g_ref[rows, :].astype(F32)).astype(BF16)
            sf_ref[...] = dec * sf_ref[...] + kv_state(k * tab_ref[0], v)


def _retention(proj, lg, gn_gain, B, S):
    nt = S // RET_TILE
    nc = S // RET_CHUNK
    qkw = RET_QK_W

    def kv_idx(blk):
        def f(b, ph, i, lg_ref):
            t = jnp.where(ph == 0, nt - 1 - i, i)
            return (b * nt + t, blk)
        return f

    def out_idx(blk):
        def f(b, ph, i, lg_ref):
            return (b * nt + i * ph, blk)
        return f

    grid_spec = pltpu.PrefetchScalarGridSpec(
        num_scalar_prefetch=1,
        grid=(B, 2, nt),
        in_specs=[
            pl.BlockSpec((RET_TILE, qkw), out_idx(2 * CB_RQK)),
            pl.BlockSpec((RET_TILE, qkw), kv_idx(2 * CB_RQK + 1)),
            pl.BlockSpec((RET_TILE, RET_V_W), kv_idx(CB_RV)),
            pl.BlockSpec((RET_TILE, RET_V_W), out_idx(CB_RG)),
            pl.BlockSpec((1, RET_V_W), lambda b, ph, i, lg_ref: (0, 0)),
        ],
        out_specs=pl.BlockSpec((RET_TILE, RET_V_W), out_idx(0)),
        scratch_shapes=[
            pltpu.VMEM((RET_HEADS, RET_CHUNK, RET_CHUNK), F32),
            pltpu.VMEM((6, RET_CHUNK, qkw), F32),
            pltpu.VMEM((qkw, RET_V_DIM), F32),
            pltpu.VMEM((qkw, RET_V_DIM), F32),
            pltpu.VMEM((nc, qkw, RET_V_DIM), F32),
        ],
    )
    return pl.pallas_call(
        functools.partial(_retention_kernel, n_tiles=nt),
        grid_spec=grid_spec,
        out_shape=jax.ShapeDtypeStruct((B * S, RET_V_W), BF16),
        compiler_params=_cparams(("arbitrary", "arbitrary", "arbitrary")),
        name="retention",
    )(lg, proj, proj, proj, proj, gn_gain.reshape(1, RET_V_W))


ATT_QB = 128
LOG2E = 1.4426950408889634


def _dil_attn_kernel(q_ref, km_ref, kp_ref, kn_ref, vm_ref, vp_ref, vn_ref, o_ref, lse_ref, *, L, Lq, d):
    i = pl.program_id(1)
    R, QB, HD = ATT_R, ATT_QB, ATT_HEAD_DIM
    nq = Lq // QB
    NK = QB + 2 * R
    row = lax.broadcasted_iota(I32, (QB, NK), 0)
    col = lax.broadcasted_iota(I32, (QB, NK), 1)
    band = jnp.logical_and(col - row >= 0, col - row <= 2 * R)
    lane = lax.broadcasted_iota(I32, (QB, LANES), 1)
    scale = HD ** -0.5

    def rows(ref, r, n):
        return ref[pl.ds(r, n, stride=d), :] if d > 1 else ref[...]

    def residue(r):
        q_all = rows(q_ref, r, Lq)
        k_all = jnp.concatenate([rows(kp_ref, r, R), rows(km_ref, r, Lq), rows(kn_ref, r, R)], axis=0)
        v_all = jnp.concatenate([rows(vp_ref, r, R), rows(vm_ref, r, Lq), rows(vn_ref, r, R)], axis=0)
        for qb in range(nq):
            base = i * Lq + qb * QB - R
            ok = jnp.logical_and(band, jnp.logical_and(col >= -base, col < L - base))
            q2 = _unpack_halves(q_all[qb * QB:(qb + 1) * QB])
            k2 = _unpack_halves(k_all[qb * QB:qb * QB + NK])
            v2 = _unpack_halves(v_all[qb * QB:qb * QB + NK])
            outs, lses = [], []
            for half in range(2):
                s = lax.dot_general(q2[half].astype(BF16), k2[half].astype(BF16), (((1,), (1,)), ((), ())),
                                    preferred_element_type=F32)
                s = jnp.where(ok, s, NEG_BIG)
                m = jnp.max(s, axis=-1, keepdims=True)
                p = jnp.exp2((s - m) * (scale * LOG2E))
                den = jnp.sum(p, axis=-1, keepdims=True)
                pn = (p * (1.0 / den)).astype(BF16)
                outs.append(jnp.dot(pn, v2[half].astype(BF16), preferred_element_type=F32))
                lses.append(m * scale + jnp.log(den))
            packed = _pack_halves(jnp.concatenate(outs, axis=1))
            lse_tile = jnp.where(lane < LANES // 2, lses[0], lses[1])
            dst = pl.ds(r + qb * QB * d, QB, stride=d) if d > 1 else pl.ds(qb * QB, QB)
            o_ref[dst, :] = packed
            lse_ref[dst, :] = lse_tile

    if d > 1:
        def body(r, carry):
            residue(r)
            return carry

        lax.fori_loop(0, d, body, 0, unroll=4)
    else:
        residue(0)


def _dil_attn(attp, g, B, S):
    window, d = ATT_PATTERNS[g]
    L = S // d
    Lq = min(L, 1024 if d == 1 else 256)
    tb = Lq * d
    nb = S // tb
    halo = ATT_R * d
    hb = tb // halo
    nhb = S // halo
    npair = ATTP_BLK // LANES
    cq, ck, cv = (npair * (3 * g + kind) for kind in range(3))

    def main(c):
        return pl.BlockSpec((tb, LANES), lambda b, i, p: (b * nb + i, c + p))

    def prev(c):
        return pl.BlockSpec((halo, LANES), lambda b, i, p: (b * nhb + jnp.maximum(i * hb - 1, 0), c + p))

    def nxt(c):
        return pl.BlockSpec((halo, LANES), lambda b, i, p: (b * nhb + jnp.minimum((i + 1) * hb, nhb - 1), c + p))

    out_spec = pl.BlockSpec((tb, LANES), lambda b, i, p: (b * nb + i, p))
    return pl.pallas_call(
        functools.partial(_dil_attn_kernel, L=L, Lq=Lq, d=d),
        grid=(B, nb, npair),
        in_specs=[main(cq), main(ck), prev(ck), nxt(ck), main(cv), prev(cv), nxt(cv)],
        out_specs=[out_spec, out_spec],
        out_shape=[
            jax.ShapeDtypeStruct((B * S, npair * LANES), U32),
            jax.ShapeDtypeStruct((B * S, npair * LANES), F32),
        ],
        compiler_params=_cparams(("arbitrary", "arbitrary", "arbitrary")),
        name=f"dil_attn_{g}",
    )(attp, attp, attp, attp, attp, attp, attp)


MERGE_TILE = 512
MERGE_ROWS = 512


def _merge_kernel(x_ref, r_ref, o0_ref, o1_ref, o2_ref, l0_ref, l1_ref, l2_ref, su_ref, sv_ref,
                  g0_ref, g1_ref, g2_ref, wr_ref, wa_ref, ws_ref, wo_ref, sgw_ref, sgb_ref, lng_ref,
                  nf_ref, wrt_ref, brt_ref, tri_ref,
                  x2_ref, hp_ref, route_ref, cnt_ref, carry_ref):
    i = pl.program_id(0)
    tm = x_ref.shape[0]
    HD = ATT_HEAD_DIM

    @pl.when(i == 0)
    def _():
        carry_ref[...] = jnp.zeros_like(carry_ref)

    def row_chunk(rows, carry):
        l0, l1, l2 = l0_ref[rows, :], l1_ref[rows, :], l2_ref[rows, :]
        m = jnp.maximum(jnp.maximum(l0, l1), l2)
        e0, e1, e2 = jnp.exp(l0 - m), jnp.exp(l1 - m), jnp.exp(l2 - m)
        inv = 1.0 / (e0 + e1 + e2)
        wts = (e0 * inv, e1 * inv, e2 * inv)
        outs = [jnp.concatenate(_unpack_halves(o_ref[rows, :]), axis=1) for o_ref in (o0_ref, o1_ref, o2_ref)]
        heads = []
        for h in range(ATT_HEADS):
            cols = slice(h * HD, (h + 1) * HD)
            c = (h % 2) * LANES + (h // 2) * (LANES // 2)
            heads.append(sum(w[:, c:c + 1] * o[:, cols] for w, o in zip(wts, outs)))
        att = jnp.concatenate(heads, axis=1).astype(BF16)
        y_att = jnp.dot(att, wa_ref[...], preferred_element_type=F32)
        y_ret = jnp.dot(r_ref[rows, :], wr_ref[...], preferred_element_type=F32)

        u = su_ref[rows, :].astype(F32)
        v = sv_ref[rows, :].astype(F32)
        mu = jnp.mean(v, axis=-1, keepdims=True)
        var = jnp.mean(jnp.square(v - mu), axis=-1, keepdims=True)
        vn = ((v - mu) * lax.rsqrt(var + NORM_EPS) * lng_ref[...]).astype(BF16)
        gw = SG_WIDTH // SG_GROUPS
        sg_chunks = []
        for c in range(MERGE_ROWS // SG_CHUNK):
            sub = slice(c * SG_CHUNK, (c + 1) * SG_CHUNK)
            parts = [jnp.dot(sgw_ref[g], vn[sub, g * gw:(g + 1) * gw], preferred_element_type=F32)
                     for g in range(SG_GROUPS)]
            sg_chunks.append(jnp.concatenate(parts, axis=1) + sgb_ref[...])
        vs = jnp.concatenate(sg_chunks, axis=0)
        y_sg = jnp.dot((u * vs).astype(BF16), ws_ref[...], preferred_element_type=F32)

        merged = (g0_ref[rows, :].astype(F32) * y_ret + g1_ref[rows, :].astype(F32) * y_att
                  + g2_ref[rows, :].astype(F32) * y_sg)
        x2 = x_ref[rows, :] + jnp.dot(merged.astype(BF16), wo_ref[...], preferred_element_type=F32)
        x2_ref[rows, :] = x2

        ms = jnp.mean(x2 * x2, axis=-1, keepdims=True)
        h2 = x2 * lax.rsqrt(ms + NORM_EPS) * nf_ref[...]
        hp_ref[rows, :] = _pack_halves(h2)

        h_hi = h2.astype(BF16)
        h_lo = (h2 - h_hi.astype(F32)).astype(BF16)
        hw = jnp.dot(h_hi, wrt_ref[...], preferred_element_type=F32)
        logits = (hw[:, :LANES] + hw[:, LANES:] + jnp.dot(h_lo, wrt_ref[:, :LANES], preferred_element_type=F32)
                  + brt_ref[...])
        lane = lax.broadcasted_iota(I32, logits.shape, 1)
        big = jnp.int32(LANES)
        is_g = lane < N_GROUPS
        gl = jnp.where(is_g, logits, -jnp.inf)
        gmax = jnp.max(gl, axis=-1, keepdims=True)
        g_idx = jnp.min(jnp.where(jnp.logical_and(is_g, gl == gmax), lane, big), axis=-1, keepdims=True)
        g_w = 1.0 / jnp.sum(jnp.where(is_g, jnp.exp(gl - gmax), 0.0), axis=-1, keepdims=True)
        lo = ROUTE_LANE0 + g_idx * EXP_PER_GROUP
        in_grp = jnp.logical_and(lane >= lo, lane < lo + EXP_PER_GROUP)
        el = jnp.where(in_grp, logits, -jnp.inf)
        v1 = jnp.max(el, axis=-1, keepdims=True)
        i1 = jnp.min(jnp.where(el == v1, lane, big), axis=-1, keepdims=True)
        el2 = jnp.where(lane == i1, -jnp.inf, el)
        v2 = jnp.max(el2, axis=-1, keepdims=True)
        i2 = jnp.min(jnp.where(el2 == v2, lane, big), axis=-1, keepdims=True)
        t = jnp.exp(v2 - v1)
        w1 = g_w / (1.0 + t)
        w2 = g_w * t / (1.0 + t)

        oh1 = lane == i1
        oh2 = lane == i2
        both = jnp.where(jnp.logical_or(oh1, oh2), 1.0, 0.0)
        before = jnp.dot(tri_ref[...], both.astype(BF16), preferred_element_type=F32) + carry
        rank1 = jnp.sum(jnp.where(oh1, before, 0.0), axis=-1, keepdims=True)
        rank2 = jnp.sum(jnp.where(oh2, before, 0.0), axis=-1, keepdims=True)

        out = jnp.where(lane == RT_EID, (i1 - ROUTE_LANE0).astype(F32), 0.0)
        out = jnp.where(lane == RT_EID + 1, (i2 - ROUTE_LANE0).astype(F32), out)
        out = jnp.where(lane == RT_W, w1, out)
        out = jnp.where(lane == RT_W + 1, w2, out)
        out = jnp.where(lane == RT_RANK, rank1, out)
        out = jnp.where(lane == RT_RANK + 1, rank2, out)
        route_ref[rows, :] = out
        return carry + jnp.sum(both, axis=0, keepdims=True)

    carry = carry_ref[0:1, :]
    for c in range(tm // MERGE_ROWS):
        carry = row_chunk(pl.ds(c * MERGE_ROWS, MERGE_ROWS), carry)
    carry_ref[...] = jnp.broadcast_to(carry, carry_ref.shape)
    cnt_ref[...] = jnp.broadcast_to(carry, cnt_ref.shape)


def _merge(x2d, r, os_, lses, proj, lw):
    T = x2d.shape[0]
    tm = MERGE_TILE
    row = lambda w: pl.BlockSpec((tm, w), lambda i: (i, 0))
    pcol = lambda w, cb: pl.BlockSpec((tm, w), lambda i: (i, cb))
    full = lambda a: pl.BlockSpec(a.shape, lambda i: (0,) * a.ndim)
    gate_cb = CB_GATE * COL_BLK // D_MODEL
    weights = [lw["w_ret_out"], lw["w_att_out"], lw["w_sg_out"], lw["w_o"], lw["sg_w"], lw["sg_b_full"],
               lw["sg_ln_gain"], lw["norm_ffn"], lw["w_router"], lw["b_router"], lw["tri"]]
    return pl.pallas_call(
        _merge_kernel,
        grid=(T // tm,),
        in_specs=[row(D_MODEL), row(RET_V_W)] + [row(2 * LANES)] * (2 * len(ATT_PATTERNS)) + [
                  pcol(SG_WIDTH, CB_SGU), pcol(SG_WIDTH, CB_SGV),
                  pcol(D_MODEL, gate_cb), pcol(D_MODEL, gate_cb + 1), pcol(D_MODEL, gate_cb + 2)
                 ] + [full(a) for a in weights],
        out_specs=[row(D_MODEL), row(D_MODEL // 2), row(LANES), pl.BlockSpec((8, LANES), lambda i: (0, 0))],
        out_shape=[jax.ShapeDtypeStruct((T, D_MODEL), F32), jax.ShapeDtypeStruct((T, D_MODEL // 2), U32),
                   jax.ShapeDtypeStruct((T, LANES), F32), jax.ShapeDtypeStruct((8, LANES), F32)],
        scratch_shapes=[pltpu.VMEM((8, LANES), F32)],
        compiler_params=_cparams(("arbitrary",)),
        name="merge",
    )(x2d, r, *os_, *lses, proj, proj, proj, proj, proj, *weights)


DISPATCH_TILE = 1024
ISSUE_UNROLL = 8


def _dispatch_kernel(dest_ref, h_ref, xb_in_ref, xb_ref, sem):
    del xb_in_ref
    tm = DISPATCH_TILE

    def issue(g, carry):
        base = pl.multiple_of(g * ISSUE_UNROLL, ISSUE_UNROLL)
        for u in range(ISSUE_UNROLL):
            src = h_ref.at[pl.ds(base + u, 1)]
            for k in range(TOP_K):
                pltpu.make_async_copy(src, xb_ref.at[pl.ds(dest_ref[0, 0, TOP_K * (base + u) + k], 1)], sem).start()
        return carry

    lax.fori_loop(0, tm // ISSUE_UNROLL, issue, 0)
    pltpu.make_async_copy(xb_ref.at[pl.ds(0, TOP_K * tm)], xb_ref.at[pl.ds(0, TOP_K * tm)], sem).wait()


def _dispatch(hp, dest, P):
    T = hp.shape[0]
    tm = DISPATCH_TILE
    half = D_MODEL // 2
    return pl.pallas_call(
        _dispatch_kernel,
        grid=(T // tm,),
        in_specs=[
            pl.BlockSpec((1, 1, TOP_K * tm), lambda i: (i, 0, 0), memory_space=pltpu.SMEM),
            pl.BlockSpec((tm, half), lambda i: (i, 0)),
            pl.BlockSpec(memory_space=pl.ANY),
        ],
        out_specs=pl.BlockSpec(memory_space=pl.ANY),
        out_shape=jax.ShapeDtypeStruct((P, half), U32),
        scratch_shapes=[pltpu.SemaphoreType.DMA(())],
        input_output_aliases={2: 0},
        compiler_params=_cparams(("arbitrary",)),
        name="dispatch",
    )(dest.reshape(T // tm, 1, TOP_K * tm), hp, jnp.zeros((P, half), U32))


def _experts_kernel(blk_e_ref, nused_ref, x_ref, wg_ref, wu_ref, wd_ref, y_ref, wg_s, wu_s, wd_s):
    i = pl.program_id(0)

    @pl.when(jnp.logical_or(i == 0, blk_e_ref[i] != blk_e_ref[jnp.maximum(i - 1, 0)]))
    def _():
        wg_s[...] = wg_ref[0, 0].astype(BF16)
        wu_s[...] = wu_ref[0, 0].astype(BF16)
        wd_s[...] = wd_ref[0, 0].astype(BF16)

    @pl.when(i < nused_ref[0])
    def _():
        x = jnp.concatenate(_unpack_halves(x_ref[...]), axis=1).astype(BF16)
        gate = jnp.dot(x, wg_s[...], preferred_element_type=F32)
        up = jnp.dot(x, wu_s[...], preferred_element_type=F32)
        hid = (gate * _sigmoid(gate) * up).astype(BF16)
        y_ref[...] = jnp.dot(hid, wd_s[...], preferred_element_type=F32)

    @pl.when(i >= nused_ref[0])
    def _():
        y_ref[...] = jnp.zeros_like(y_ref)


def _experts(xb, blk_e, nused, wg, wu, wd, layer):
    P = xb.shape[0]
    bm = MOE_BLOCK
    grid_spec = pltpu.PrefetchScalarGridSpec(
        num_scalar_prefetch=2,
        grid=(P // bm,),
        in_specs=[
            pl.BlockSpec((bm, D_MODEL // 2), lambda i, e, n: (i, 0)),
            pl.BlockSpec((1, 1, D_MODEL, EXP_HIDDEN), lambda i, e, n: (layer, e[i], 0, 0)),
            pl.BlockSpec((1, 1, D_MODEL, EXP_HIDDEN), lambda i, e, n: (layer, e[i], 0, 0)),
            pl.BlockSpec((1, 1, EXP_HIDDEN, D_MODEL), lambda i, e, n: (layer, e[i], 0, 0)),
        ],
        out_specs=pl.BlockSpec((bm, D_MODEL), lambda i, e, n: (i, 0)),
        scratch_shapes=[pltpu.VMEM((D_MODEL, EXP_HIDDEN), BF16), pltpu.VMEM((D_MODEL, EXP_HIDDEN), BF16),
                        pltpu.VMEM((EXP_HIDDEN, D_MODEL), BF16)],
    )
    return pl.pallas_call(
        _experts_kernel,
        grid_spec=grid_spec,
        out_shape=jax.ShapeDtypeStruct((P, D_MODEL), F32),
        compiler_params=_cparams(("arbitrary",)),
        name="experts",
    )(blk_e, nused, xb, wg, wu, wd)


COMBINE_TILE = 256


def _combine_kernel(dest_ref, dest_next_ref, x_ref, route_ref, gain_ref, y_ref, o_ref, buf_ref, sems, *, final_norm):
    i = pl.program_id(0)
    n = pl.num_programs(0)
    tm = COMBINE_TILE

    def gather(idx_ref, slot):
        def issue(g, carry):
            base = pl.multiple_of(g * ISSUE_UNROLL, ISSUE_UNROLL)
            for u in range(ISSUE_UNROLL):
                for k in range(TOP_K):
                    pltpu.make_async_copy(y_ref.at[pl.ds(idx_ref[0, 0, TOP_K * (base + u) + k], 1)],
                                          buf_ref.at[slot, k, pl.ds(base + u, 1)], sems.at[slot]).start()
            return carry

        lax.fori_loop(0, tm // ISSUE_UNROLL, issue, 0)

    @pl.when(i == 0)
    def _():
        gather(dest_ref, 0)

    def step(slot):
        @pl.when(i + 1 < n)
        def _():
            gather(dest_next_ref, 1 - slot)

        pltpu.make_async_copy(buf_ref.at[slot], buf_ref.at[slot], sems.at[slot]).wait()
        route = route_ref[...]
        x = x_ref[...] + sum(route[:, RT_W + k:RT_W + k + 1] * buf_ref[slot, k] for k in range(TOP_K))
        if final_norm:
            ms = jnp.mean(x * x, axis=-1, keepdims=True)
            x = x * lax.rsqrt(ms + NORM_EPS) * gain_ref[...]
        o_ref[...] = x

    for slot in range(2):
        pl.when(i % 2 == slot)(functools.partial(step, slot))


def _combine(x2, route, yb, dest, gain, final_norm):
    T = x2.shape[0]
    tm = COMBINE_TILE
    nt = T // tm
    dest3 = dest.reshape(nt, 1, TOP_K * tm)
    return pl.pallas_call(
        functools.partial(_combine_kernel, final_norm=final_norm),
        grid=(nt,),
        in_specs=[
            pl.BlockSpec((1, 1, TOP_K * tm), lambda i: (i, 0, 0), memory_space=pltpu.SMEM),
            pl.BlockSpec((1, 1, TOP_K * tm), lambda i: (jnp.minimum(i + 1, nt - 1), 0, 0), memory_space=pltpu.SMEM),
            pl.BlockSpec((tm, D_MODEL), lambda i: (i, 0)),
            pl.BlockSpec((tm, LANES), lambda i: (i, 0)),
            pl.BlockSpec((1, D_MODEL), lambda i: (0, 0)),
            pl.BlockSpec(memory_space=pl.ANY),
        ],
        out_specs=pl.BlockSpec((tm, D_MODEL), lambda i: (i, 0)),
        out_shape=jax.ShapeDtypeStruct((T, D_MODEL), F32),
        scratch_shapes=[pltpu.VMEM((2, TOP_K, tm, D_MODEL), F32), pltpu.SemaphoreType.DMA((2,))],
        compiler_params=_cparams(("arbitrary",)),
        name="combine_final" if final_norm else "combine",
    )(dest3, dest3, x2, route, gain.reshape(1, D_MODEL), yb)


def _layer_weights(l, p):
    w_router = jnp.zeros((D_MODEL, LANES), F32)
    w_router = w_router.at[:, :N_GROUPS].set(p["w_router_group"][l])
    w_router = w_router.at[:, ROUTE_LANE0:ROUTE_LANE0 + N_EXPERTS].set(p["w_router_expert"][l])
    b_router = jnp.zeros((1, LANES), F32)
    b_router = b_router.at[0, :N_GROUPS].set(p["b_router_group"][l])
    b_router = b_router.at[0, ROUTE_LANE0:ROUTE_LANE0 + N_EXPERTS].set(p["b_router_expert"][l])
    tri = (jnp.arange(MERGE_ROWS)[:, None] > jnp.arange(MERGE_ROWS)[None, :]).astype(BF16)
    gw = SG_WIDTH // SG_GROUPS
    w_in, w_in_rot = _in_proj_weights(p["w_in"][l])
    return {
        "norm_mix": p["norm_mix"][l],
        "w_in": w_in,
        "w_in_rot": w_in_rot,
        "lg": jnp.stack([-jnp.exp(p["ret_decay_fwd"][l].astype(F32)), -jnp.exp(p["ret_decay_bwd"][l].astype(F32))]),
        "ret_gn_gain": p["ret_gn_gain"][l],
        "w_ret_out": p["w_ret_out"][l].astype(BF16),
        "w_att_out": p["w_att_out"][l].astype(BF16),
        "w_sg_out": p["w_sg_out"][l].astype(BF16),
        "w_o": p["w_o"][l].astype(BF16),
        "sg_w": p["sg_w"][l].astype(BF16),
        "sg_b_full": jnp.repeat(p["sg_b"][l].T, gw, axis=1),
        "sg_ln_gain": p["sg_ln_gain"][l].reshape(1, SG_WIDTH),
        "norm_ffn": p["norm_ffn"][l].reshape(1, D_MODEL),
        "w_router": jnp.concatenate([w_router.astype(BF16),
                                     (w_router - w_router.astype(BF16).astype(F32)).astype(BF16)], axis=1),
        "b_router": b_router,
        "tri": tri,
    }


def _moe_plan(route, cnt, T):
    bm = MOE_BLOCK
    eid = route[:, RT_EID:RT_EID + TOP_K].astype(I32)
    rank = route[:, RT_RANK:RT_RANK + TOP_K].astype(I32)
    counts = cnt[0, ROUTE_LANE0:ROUTE_LANE0 + N_EXPERTS].astype(I32)
    padded = (counts + bm - 1) // bm * bm
    pad_end = jnp.cumsum(padded)
    pad_start = pad_end - padded
    experts = jnp.arange(N_EXPERTS, dtype=I32)
    dest = jnp.sum(jnp.where(eid[..., None] == experts, pad_start, 0), axis=-1) + rank
    P = T * TOP_K + N_EXPERTS * bm
    nblk = P // bm
    blk_start = jnp.arange(nblk, dtype=I32) * bm
    blk_e = jnp.minimum(jnp.sum((pad_end[None, :] <= blk_start[:, None]).astype(I32), axis=1), N_EXPERTS - 1)
    nused = (pad_end[-1:] // bm).astype(I32)
    return dest.reshape(-1), blk_e, nused, P


def _run_trunk(x, lws, w_exp, tabs, norm_final):
    B, S, _ = x.shape
    T = B * S
    x2d = x.reshape(T, D_MODEL)
    tm_in = min(2048, S)
    for l, lw in enumerate(lws):
        proj, attp = _in_proj(x2d, lw["norm_mix"], lw["w_in"], lw["w_in_rot"], tabs, S, tm_in)
        r = _retention(proj, lw["lg"], lw["ret_gn_gain"], B, S)
        os_, lses = zip(*[_dil_attn(attp, g, B, S) for g in range(len(ATT_PATTERNS))])
        x2, hp, route, cnt = _merge(x2d, r, os_, lses, proj, lw)
        dest, blk_e, nused, P = _moe_plan(route, cnt, T)
        xb = _dispatch(hp, dest, P)
        yb = _experts(xb, blk_e, nused, *w_exp, layer=l)
        x2d = _combine(x2, route, yb, dest, norm_final, final_norm=(l == len(lws) - 1))
    return x2d.reshape(B, S, D_MODEL)


def kernel(x_prompt, x_sample, norm_mix, w_in, ret_decay_fwd, ret_decay_bwd, ret_gn_gain, sg_ln_gain, sg_w, sg_b,
           w_ret_out, w_att_out, w_sg_out, w_o, norm_ffn, w_router_group, b_router_group, w_router_expert,
           b_router_expert, w_exp_gate, w_exp_up, w_exp_down, norm_final):
    p = dict(norm_mix=norm_mix, w_in=w_in, ret_decay_fwd=ret_decay_fwd, ret_decay_bwd=ret_decay_bwd,
             ret_gn_gain=ret_gn_gain, sg_ln_gain=sg_ln_gain, sg_w=sg_w, sg_b=sg_b, w_ret_out=w_ret_out,
             w_att_out=w_att_out, w_sg_out=w_sg_out, w_o=w_o, norm_ffn=norm_ffn, w_router_group=w_router_group,
             b_router_group=b_router_group, w_router_expert=w_router_expert, b_router_expert=b_router_expert)
    lws = [_layer_weights(l, p) for l in range(DEPTH)]
    s_max = max(x_prompt.shape[1], x_sample.shape[1])
    tabs = _ret_rot_tables(s_max) + _att_rot_tables(s_max)
    w_exp = (w_exp_gate, w_exp_up, w_exp_down)
    y_prompt = _run_trunk(x_prompt, lws, w_exp, tabs, norm_final)
    y_sample = _run_trunk(x_sample, lws, w_exp, tabs, norm_final)
    return (y_prompt, y_sample)
```

```python
import functools

import jax
import jax.numpy as jnp
import numpy as np
from jax import lax
from jax.experimental import pallas as pl
from jax.experimental.pallas import tpu as pltpu

F32 = jnp.float32
BF16 = jnp.bfloat16
U32 = jnp.uint32
I32 = jnp.int32

D_MODEL = 1024
DEPTH = 2
NORM_EPS = 1e-6
RET_HEADS = 4
RET_QK_DIM = 64
RET_V_DIM = 128
RET_CHUNK = 128
RET_THETA = 10000.0
ATT_PATTERNS = ((128, 1), (512, 4), (2048, 16))
ATT_HEADS = 4
ATT_HEAD_DIM = 128
ATT_ROT_DIM = ATT_HEAD_DIM // 4
ROPE_THETA = 500000.0
SG_GROUPS = 4
SG_CHUNK = 128
SG_WIDTH = 512
N_BRANCH = 3
N_GROUPS = 4
EXP_PER_GROUP = 8
N_EXPERTS = N_GROUPS * EXP_PER_GROUP
TOP_K = 2
EXP_HIDDEN = 512
RET_QK_W = RET_HEADS * RET_QK_DIM
RET_V_W = RET_HEADS * RET_V_DIM
ATT_W = ATT_HEADS * ATT_HEAD_DIM
ATT_IN_W = len(ATT_PATTERNS) * 3 * ATT_W
SG_IN_W = 2 * SG_WIDTH
GATE_IN_W = N_BRANCH * D_MODEL
N_IN = 2 * RET_QK_W + 2 * RET_V_W + ATT_IN_W + SG_IN_W + GATE_IN_W
NEG_BIG = -1e30

LANES = 128
COL_BLK = 512
N_COL_BLK = N_IN // COL_BLK
ATT_R = 64
assert all(w // (2 * d) == ATT_R for w, d in ATT_PATTERNS)
CB_GATE, CB_RV, CB_RG, CB_SGU, CB_SGV, CB_RQK, CB_ATT = 0, 6, 7, 8, 9, 10, 11
N_PROJ_BLK = CB_ATT
N_ATT_BLK = N_COL_BLK - CB_ATT
ATT_PAIR_SHIFT = LANES // 2
ATTP_BLK = COL_BLK // 2
ROUTE_LANE0 = N_GROUPS
RT_EID, RT_W, RT_RANK = 0, TOP_K, 2 * TOP_K
MOE_BLOCK = 512
VMEM_LIMIT = 56 * 1024 * 1024
HI_HALF = 0xFFFF0000
IN_PROJ_ROWS = 256


def _cparams(sem, vmem=VMEM_LIMIT):
    return pltpu.CompilerParams(dimension_semantics=sem, vmem_limit_bytes=vmem)


def _sigmoid(x):
    return 0.5 * jnp.tanh(0.5 * x) + 0.5


def _pack_halves(y):
    n = y.shape[1] // 2
    bits = lax.bitcast_convert_type(y.astype(BF16).astype(F32), U32)
    return (bits[:, :n] >> 16) | (bits[:, n:] & jnp.uint32(HI_HALF))


def _unpack_halves(p):
    return (lax.bitcast_convert_type(p << 16, F32), lax.bitcast_convert_type(p & jnp.uint32(HI_HALF), F32))


def _in_proj_kernel(x_ref, g_ref, w_ref, wrot_ref, rc_ref, rs_ref, ac_ref, as_ref, o_ref, oa_ref, h_ref):
    j = pl.program_id(1)

    chunks = [pl.ds(r * IN_PROJ_ROWS, IN_PROJ_ROWS) for r in range(x_ref.shape[0] // IN_PROJ_ROWS)]

    @pl.when(j == 0)
    def _():
        for rows in chunks:
            x = x_ref[rows, :]
            ms = jnp.mean(x * x, axis=-1, keepdims=True)
            h = (x * lax.rsqrt(ms + NORM_EPS) * g_ref[...]).astype(BF16)
            h_ref[rows, :] = h
            o_ref[rows, :] = _sigmoid(jnp.dot(h, w_ref[...], preferred_element_type=F32)).astype(BF16)

    def dot(w, rows):
        return jnp.dot(h_ref[rows, :], w[...], preferred_element_type=F32)

    reps = COL_BLK // LANES

    def tiled(tab_ref, rows):
        return jnp.concatenate([tab_ref[rows, :]] * reps, axis=1)

    att_kind = (j - CB_ATT) % 3

    @pl.when(jnp.logical_and(j > 0, j < CB_RV))
    def _():
        for rows in chunks:
            o_ref[rows, :] = _sigmoid(dot(w_ref, rows)).astype(BF16)

    @pl.when(j == CB_RV)
    def _():
        for rows in chunks:
            o_ref[rows, :] = dot(w_ref, rows).astype(BF16)

    @pl.when(j == CB_RG)
    def _():
        for rows in chunks:
            acc = dot(w_ref, rows)
            o_ref[rows, :] = (acc * _sigmoid(acc)).astype(BF16)

    @pl.when(jnp.logical_or(j == CB_SGU, j == CB_SGV))
    def _():
        for rows in chunks:
            o_ref[rows, :] = jax.nn.gelu(dot(w_ref, rows)).astype(BF16)

    @pl.when(j == CB_RQK)
    def _():
        for rows in chunks:
            y = dot(w_ref, rows) * tiled(rc_ref, rows) + dot(wrot_ref, rows) * tiled(rs_ref, rows)
            lane = lax.broadcasted_iota(I32, y.shape, 1)
            y = jnp.where(lane >= RET_QK_W, y * (RET_QK_DIM ** -0.5), y)
            o_ref[rows, :] = y.astype(BF16)

    @pl.when(jnp.logical_and(j >= CB_ATT, att_kind != 2))
    def _():
        for rows in chunks:
            acc = dot(w_ref, rows)
            partner = jnp.concatenate([pltpu.roll(acc[:, c * LANES:(c + 1) * LANES], ATT_PAIR_SHIFT, 1)
                                       for c in range(reps)], axis=1)
            oa_ref[rows, :] = _pack_halves(acc * tiled(ac_ref, rows) + partner * tiled(as_ref, rows))

    @pl.when(jnp.logical_and(j >= CB_ATT, att_kind == 2))
    def _():
        for rows in chunks:
            oa_ref[rows, :] = _pack_halves(dot(w_ref, rows))


def _rot_angles(S, n_rot, theta):
    half = n_rot // 2
    inv = jnp.power(jnp.float32(theta), -jnp.arange(half, dtype=F32) * (2.0 / n_rot))
    ang = jnp.arange(S, dtype=F32)[:, None] * inv[None, :]
    return jnp.cos(ang), jnp.sin(ang)


def _ret_rot_tables(S):
    cos, sin = _rot_angles(S, RET_QK_DIM, RET_THETA)
    return jnp.tile(cos, (1, LANES // cos.shape[1])), jnp.tile(sin, (1, LANES // sin.shape[1]))


def _att_rot_tables(S):
    cos, sin = _rot_angles(S, ATT_ROT_DIM, ROPE_THETA)
    half = ATT_ROT_DIM // 2
    gap = jnp.ones((S, ATT_PAIR_SHIFT - half), F32)
    cos_h = jnp.concatenate([cos, gap, cos, gap], axis=1)
    sin_h = jnp.concatenate([-sin, 0.0 * gap, sin, 0.0 * gap], axis=1)
    return cos_h, sin_h


def _att_head_order(w):
    half = ATT_ROT_DIM // 2
    k, n = w.shape
    wh = w.reshape(k, n // ATT_HEAD_DIM, ATT_HEAD_DIM)
    s = ATT_PAIR_SHIFT
    out = jnp.concatenate([wh[..., :half], wh[..., s:s + half], wh[..., 2 * half:s], wh[..., half:2 * half],
                           wh[..., s + half:]], axis=-1)
    return out.reshape(k, n)


def _rot_partner_cols(w, n_rot, period):
    half = n_rot // 2
    k, n = w.shape
    wh = w.reshape(k, n // period, period)
    out = jnp.concatenate([-wh[..., half:n_rot], wh[..., :half], jnp.zeros_like(wh[..., n_rot:])], axis=-1)
    return out.reshape(k, n)


def _in_proj_weights(w_in):
    o = np.cumsum([0, RET_QK_W, RET_QK_W, RET_V_W, RET_V_W, ATT_IN_W, SG_IN_W, GATE_IN_W])
    rqk, rv, rg, att, sg, gate = (w_in[:, o[0]:o[2]], w_in[:, o[2]:o[3]], w_in[:, o[3]:o[4]], w_in[:, o[4]:o[5]],
                                  w_in[:, o[5]:o[6]], w_in[:, o[6]:o[7]])
    att_blocks = []
    for blk in range(N_ATT_BLK):
        cols = att[:, blk * ATT_W:(blk + 1) * ATT_W]
        att_blocks.append(cols if blk % 3 == 2 else _att_head_order(cols))
    w = jnp.concatenate([gate, rv, rg, sg, rqk] + att_blocks, axis=1)
    return w.astype(BF16), _rot_partner_cols(rqk, RET_QK_DIM, RET_QK_DIM).astype(BF16)


def _in_proj(x2d, gain, w_bf16, wrot_bf16, tabs, S, tm):
    T = x2d.shape[0]
    nrep = S // tm
    tab_spec = pl.BlockSpec((tm, LANES), lambda i, j: (i % nrep, 0))

    return pl.pallas_call(
        _in_proj_kernel,
        grid=(T // tm, N_COL_BLK),
        in_specs=[
            pl.BlockSpec((tm, D_MODEL), lambda i, j: (i, 0)),
            pl.BlockSpec((1, D_MODEL), lambda i, j: (0, 0)),
            pl.BlockSpec((D_MODEL, COL_BLK), lambda i, j: (0, j)),
            pl.BlockSpec((D_MODEL, COL_BLK), lambda i, j: (0, 0)),
            tab_spec, tab_spec, tab_spec, tab_spec,
        ],
        out_specs=[
            pl.BlockSpec((tm, COL_BLK), lambda i, j: (i, jnp.minimum(j, N_PROJ_BLK - 1))),
            pl.BlockSpec((tm, ATTP_BLK), lambda i, j: (i, jnp.maximum(j - CB_ATT, 0))),
        ],
        out_shape=[
            jax.ShapeDtypeStruct((T, N_PROJ_BLK * COL_BLK), BF16),
            jax.ShapeDtypeStruct((T, N_ATT_BLK * ATTP_BLK), U32),
        ],
        scratch_shapes=[pltpu.VMEM((tm, D_MODEL), BF16)],
        compiler_params=_cparams(("arbitrary", "arbitrary")),
        name="in_proj",
    )(x2d, gain.reshape(1, D_MODEL), w_bf16, wrot_bf16, *tabs)


RET_TILE = 1024
RET_CPT = RET_TILE // RET_CHUNK


def _retention_kernel(lg_ref, q_ref, k_ref, v_ref, g_ref, gn_ref, o_ref,
                      dmat_ref, tab_ref, sf_ref, sbrun_ref, sb_ref, *, n_tiles):
    b, ph, i = pl.program_id(0), pl.program_id(1), pl.program_id(2)
    C, H, DK, DV = RET_CHUNK, RET_HEADS, RET_QK_DIM, RET_V_DIM

    @pl.when(jnp.logical_and(b == 0, jnp.logical_and(ph == 0, i == 0)))
    def _():
        row = lax.broadcasted_iota(I32, (C, C), 0).astype(F32)
        col = lax.broadcasted_iota(I32, (C, C), 1).astype(F32)
        diff = row - col
        for h in range(H):
            fwd = jnp.exp(jnp.maximum(diff, 0.0) * lg_ref[0, h])
            bwd = jnp.exp(jnp.maximum(-diff, 0.0) * lg_ref[1, h])
            dmat_ref[h] = jnp.where(diff >= 0, fwd, bwd)
        pos = lax.broadcasted_iota(I32, (C, RET_QK_W), 0).astype(F32)
        head = lax.broadcasted_iota(I32, (C, RET_QK_W), 1) >> 6
        lgf = jnp.zeros((C, RET_QK_W), F32)
        lgb = jnp.zeros((C, RET_QK_W), F32)
        for h in range(H):
            lgf = jnp.where(head == h, lg_ref[0, h], lgf)
            lgb = jnp.where(head == h, lg_ref[1, h], lgb)
        tab_ref[0] = jnp.exp((C - 1.0 - pos) * lgf)
        tab_ref[1] = jnp.exp((pos + 1.0) * lgf)
        tab_ref[2] = jnp.exp(pos * lgb)
        tab_ref[3] = jnp.exp((C - pos) * lgb)
        tab_ref[4] = jnp.exp(C * lgf)
        tab_ref[5] = jnp.exp(C * lgb)

    def chunk_decay(idx):
        return tab_ref[idx][:DV, :].T

    def kv_state(kd, v):
        kdt = kd.T.astype(BF16)
        parts = [jnp.dot(kdt[h * DK:(h + 1) * DK, :], v[:, h * DV:(h + 1) * DV], preferred_element_type=F32)
                 for h in range(H)]
        return jnp.concatenate(parts, axis=0)

    @pl.when(ph == 0)
    def _():
        @pl.when(i == 0)
        def _():
            sbrun_ref[...] = jnp.zeros_like(sbrun_ref)

        tile = n_tiles - 1 - i
        dec = chunk_decay(5)
        for c in reversed(range(RET_CPT)):
            rows = pl.ds(c * C, C)
            sb_ref[tile * RET_CPT + c] = sbrun_ref[...]
            kd = k_ref[rows, :].astype(F32) * tab_ref[2]
            sbrun_ref[...] = dec * sbrun_ref[...] + kv_state(kd, v_ref[rows, :])

    @pl.when(ph == 1)
    def _():
        @pl.when(i == 0)
        def _():
            sf_ref[...] = jnp.zeros_like(sf_ref)

        dec = chunk_decay(4)
        head = lax.broadcasted_iota(I32, (C, RET_QK_W), 1) >> 6
        for c in range(RET_CPT):
            rows = pl.ds(c * C, C)
            q = q_ref[rows, :].astype(F32)
            k = k_ref[rows, :].astype(F32)
            v = v_ref[rows, :]
            kt = k.T.astype(BF16)
            qf = q * tab_ref[1]
            qb = q * tab_ref[3]
            sf = sf_ref[...].astype(BF16)
            sb = sb_ref[i * RET_CPT + c].astype(BF16)
            outs = []
            for h in range(H):
                sel = head == h
                qh = jnp.where(sel, q, 0.0).astype(BF16)
                s = jnp.dot(qh, kt, preferred_element_type=F32) * dmat_ref[h]
                o = jnp.dot(s.astype(BF16), v[:, h * DV:(h + 1) * DV], preferred_element_type=F32)
                o += jnp.dot(jnp.where(sel, qf, 0.0).astype(BF16), sf, preferred_element_type=F32)
                o += jnp.dot(jnp.where(sel, qb, 0.0).astype(BF16), sb, preferred_element_type=F32)
                mu = jnp.mean(o, axis=-1, keepdims=True)
                var = jnp.mean(jnp.square(o - mu), axis=-1, keepdims=True)
                outs.append((o - mu) * lax.rsqrt(var + NORM_EPS))
            y = jnp.concatenate(outs, axis=1) * gn_ref[...]
            o_ref[rows, :] = (y * g_ref[rows, :].astype(F32)).astype(BF16)
            sf_ref[...] = dec * sf_ref[...] + kv_state(k * tab_ref[0], v)


def _retention(proj, lg, gn_gain, B, S):
    nt = S // RET_TILE
    nc = S // RET_CHUNK
    qkw = RET_QK_W

    def kv_idx(blk):
        def f(b, ph, i, lg_ref):
            t = jnp.where(ph == 0, nt - 1 - i, i)
            return (b * nt + t, blk)
        return f

    def out_idx(blk):
        def f(b, ph, i, lg_ref):
            return (b * nt + i * ph, blk)
        return f

    grid_spec = pltpu.PrefetchScalarGridSpec(
        num_scalar_prefetch=1,
        grid=(B, 2, nt),
        in_specs=[
            pl.BlockSpec((RET_TILE, qkw), out_idx(2 * CB_RQK)),
            pl.BlockSpec((RET_TILE, qkw), kv_idx(2 * CB_RQK + 1)),
            pl.BlockSpec((RET_TILE, RET_V_W), kv_idx(CB_RV)),
            pl.BlockSpec((RET_TILE, RET_V_W), out_idx(CB_RG)),
            pl.BlockSpec((1, RET_V_W), lambda b, ph, i, lg_ref: (0, 0)),
        ],
        out_specs=pl.BlockSpec((RET_TILE, RET_V_W), out_idx(0)),
        scratch_shapes=[
            pltpu.VMEM((RET_HEADS, RET_CHUNK, RET_CHUNK), F32),
            pltpu.VMEM((6, RET_CHUNK, qkw), F32),
            pltpu.VMEM((qkw, RET_V_DIM), F32),
            pltpu.VMEM((qkw, RET_V_DIM), F32),
            pltpu.VMEM((nc, qkw, RET_V_DIM), F32),
        ],
    )
    return pl.pallas_call(
        functools.partial(_retention_kernel, n_tiles=nt),
        grid_spec=grid_spec,
        out_shape=jax.ShapeDtypeStruct((B * S, RET_V_W), BF16),
        compiler_params=_cparams(("arbitrary", "arbitrary", "arbitrary")),
        name="retention",
    )(lg, proj, proj, proj, proj, gn_gain.reshape(1, RET_V_W))


ATT_QB = 128
LOG2E = 1.4426950408889634


def _dil_attn_kernel(q_ref, km_ref, kp_ref, kn_ref, vm_ref, vp_ref, vn_ref, o_ref, lse_ref, *, L, Lq, d):
    i = pl.program_id(1)
    R, QB, HD = ATT_R, ATT_QB, ATT_HEAD_DIM
    nq = Lq // QB
    NK = QB + 2 * R
    row = lax.broadcasted_iota(I32, (QB, NK), 0)
    col = lax.broadcasted_iota(I32, (QB, NK), 1)
    band = jnp.logical_and(col - row >= 0, col - row <= 2 * R)
    lane = lax.broadcasted_iota(I32, (QB, LANES), 1)
    scale = HD ** -0.5

    def rows(ref, r, n):
        return ref[pl.ds(r, n, stride=d), :] if d > 1 else ref[...]

    def residue(r):
        q_all = rows(q_ref, r, Lq)
        k_all = jnp.concatenate([rows(kp_ref, r, R), rows(km_ref, r, Lq), rows(kn_ref, r, R)], axis=0)
        v_all = jnp.concatenate([rows(vp_ref, r, R), rows(vm_ref, r, Lq), rows(vn_ref, r, R)], axis=0)
        for qb in range(nq):
            base = i * Lq + qb * QB - R
            ok = jnp.logical_and(band, jnp.logical_and(col >= -base, col < L - base))
            q2 = _unpack_halves(q_all[qb * QB:(qb + 1) * QB])
            k2 = _unpack_halves(k_all[qb * QB:qb * QB + NK])
            v2 = _unpack_halves(v_all[qb * QB:qb * QB + NK])
            outs, lses = [], []
            for half in range(2):
                s = lax.dot_general(q2[half].astype(BF16), k2[half].astype(BF16), (((1,), (1,)), ((), ())),
                                    preferred_element_type=F32)
                s = jnp.where(ok, s, NEG_BIG)
                m = jnp.max(s, axis=-1, keepdims=True)
                p = jnp.exp2((s - m) * (scale * LOG2E))
                den = jnp.sum(p, axis=-1, keepdims=True)
                pn = (p * (1.0 / den)).astype(BF16)
                outs.append(jnp.dot(pn, v2[half].astype(BF16), preferred_element_type=F32))
                lses.append(m * scale + jnp.log(den))
            packed = _pack_halves(jnp.concatenate(outs, axis=1))
            lse_tile = jnp.where(lane < LANES // 2, lses[0], lses[1])
            dst = pl.ds(r + qb * QB * d, QB, stride=d) if d > 1 else pl.ds(qb * QB, QB)
            o_ref[dst, :] = packed
            lse_ref[dst, :] = lse_tile

    if d > 1:
        def body(r, carry):
            residue(r)
            return carry

        lax.fori_loop(0, d, body, 0, unroll=4)
    else:
        residue(0)


def _dil_attn(attp, g, B, S):
    window, d = ATT_PATTERNS[g]
    L = S // d
    Lq = min(L, 1024 if d == 1 else 256)
    tb = Lq * d
    nb = S // tb
    halo = ATT_R * d
    hb = tb // halo
    nhb = S // halo
    npair = ATTP_BLK // LANES
    cq, ck, cv = (npair * (3 * g + kind) for kind in range(3))

    def main(c):
        return pl.BlockSpec((tb, LANES), lambda b, i, p: (b * nb + i, c + p))

    def prev(c):
        return pl.BlockSpec((halo, LANES), lambda b, i, p: (b * nhb + jnp.maximum(i * hb - 1, 0), c + p))

    def nxt(c):
        return pl.BlockSpec((halo, LANES), lambda b, i, p: (b * nhb + jnp.minimum((i + 1) * hb, nhb - 1), c + p))

    out_spec = pl.BlockSpec((tb, LANES), lambda b, i, p: (b * nb + i, p))
    return pl.pallas_call(
        functools.partial(_dil_attn_kernel, L=L, Lq=Lq, d=d),
        grid=(B, nb, npair),
        in_specs=[main(cq), main(ck), prev(ck), nxt(ck), main(cv), prev(cv), nxt(cv)],
        out_specs=[out_spec, out_spec],
        out_shape=[
            jax.ShapeDtypeStruct((B * S, npair * LANES), U32),
            jax.ShapeDtypeStruct((B * S, npair * LANES), F32),
        ],
        compiler_params=_cparams(("arbitrary", "arbitrary", "arbitrary")),
        name=f"dil_attn_{g}",
    )(attp, attp, attp, attp, attp, attp, attp)


MERGE_TILE = 512
MERGE_ROWS = 512


def _merge_kernel(x_ref, r_ref, o0_ref, o1_ref, o2_ref, l0_ref, l1_ref, l2_ref, su_ref, sv_ref,
                  g0_ref, g1_ref, g2_ref, wr_ref, wa_ref, ws_ref, wo_ref, sgw_ref, sgb_ref, lng_ref,
                  nf_ref, wrt_ref, brt_ref, tri_ref,
                  x2_ref, hp_ref, route_ref, cnt_ref, carry_ref):
    i = pl.program_id(0)
    tm = x_ref.shape[0]
    HD = ATT_HEAD_DIM

    @pl.when(i == 0)
    def _():
        carry_ref[...] = jnp.zeros_like(carry_ref)

    def row_chunk(rows, carry):
        l0, l1, l2 = l0_ref[rows, :], l1_ref[rows, :], l2_ref[rows, :]
        m = jnp.maximum(jnp.maximum(l0, l1), l2)
        e0, e1, e2 = jnp.exp(l0 - m), jnp.exp(l1 - m), jnp.exp(l2 - m)
        inv = 1.0 / (e0 + e1 + e2)
        wts = (e0 * inv, e1 * inv, e2 * inv)
        outs = [jnp.concatenate(_unpack_halves(o_ref[rows, :]), axis=1) for o_ref in (o0_ref, o1_ref, o2_ref)]
        heads = []
        for h in range(ATT_HEADS):
            cols = slice(h * HD, (h + 1) * HD)
            c = (h % 2) * LANES + (h // 2) * (LANES // 2)
            heads.append(sum(w[:, c:c + 1] * o[:, cols] for w, o in zip(wts, outs)))
        att = jnp.concatenate(heads, axis=1).astype(BF16)
        y_att = jnp.dot(att, wa_ref[...], preferred_element_type=F32)
        y_ret = jnp.dot(r_ref[rows, :], wr_ref[...], preferred_element_type=F32)

        u = su_ref[rows, :].astype(F32)
        v = sv_ref[rows, :].astype(F32)
        mu = jnp.mean(v, axis=-1, keepdims=True)
        var = jnp.mean(jnp.square(v - mu), axis=-1, keepdims=True)
        vn = ((v - mu) * lax.rsqrt(var + NORM_EPS) * lng_ref[...]).astype(BF16)
        gw = SG_WIDTH // SG_GROUPS
        sg_chunks = []
        for c in range(MERGE_ROWS // SG_CHUNK):
            sub = slice(c * SG_CHUNK, (c + 1) * SG_CHUNK)
            parts = [jnp.dot(sgw_ref[g], vn[sub, g * gw:(g + 1) * gw], preferred_element_type=F32)
                     for g in range(SG_GROUPS)]
            sg_chunks.append(jnp.concatenate(parts, axis=1) + sgb_ref[...])
        vs = jnp.concatenate(sg_chunks, axis=0)
        y_sg = jnp.dot((u * vs).astype(BF16), ws_ref[...], preferred_element_type=F32)

        merged = (g0_ref[rows, :].astype(F32) * y_ret + g1_ref[rows, :].astype(F32) * y_att
                  + g2_ref[rows, :].astype(F32) * y_sg)
        x2 = x_ref[rows, :] + jnp.dot(merged.astype(BF16), wo_ref[...], preferred_element_type=F32)
        x2_ref[rows, :] = x2

        ms = jnp.mean(x2 * x2, axis=-1, keepdims=True)
        h2 = x2 * lax.rsqrt(ms + NORM_EPS) * nf_ref[...]
        hp_ref[rows, :] = _pack_halves(h2)

        h_hi = h2.astype(BF16)
        h_lo = (h2 - h_hi.astype(F32)).astype(BF16)
        hw = jnp.dot(h_hi, wrt_ref[...], preferred_element_type=F32)
        logits = (hw[:, :LANES] + hw[:, LANES:] + jnp.dot(h_lo, wrt_ref[:, :LANES], preferred_element_type=F32)
                  + brt_ref[...])
        lane = lax.broadcasted_iota(I32, logits.shape, 1)
        big = jnp.int32(LANES)
        is_g = lane < N_GROUPS
        gl = jnp.where(is_g, logits, -jnp.inf)
        gmax = jnp.max(gl, axis=-1, keepdims=True)
        g_idx = jnp.min(jnp.where(jnp.logical_and(is_g, gl == gmax), lane, big), axis=-1, keepdims=True)
        g_w = 1.0 / jnp.sum(jnp.where(is_g, jnp.exp(gl - gmax), 0.0), axis=-1, keepdims=True)
        lo = ROUTE_LANE0 + g_idx * EXP_PER_GROUP
        in_grp = jnp.logical_and(lane >= lo, lane < lo + EXP_PER_GROUP)
        el = jnp.where(in_grp, logits, -jnp.inf)
        v1 = jnp.max(el, axis=-1, keepdims=True)
        i1 = jnp.min(jnp.where(el == v1, lane, big), axis=-1, keepdims=True)
        el2 = jnp.where(lane == i1, -jnp.inf, el)
        v2 = jnp.max(el2, axis=-1, keepdims=True)
        i2 = jnp.min(jnp.where(el2 == v2, lane, big), axis=-1, keepdims=True)
        t = jnp.exp(v2 - v1)
        w1 = g_w / (1.0 + t)
        w2 = g_w * t / (1.0 + t)

        oh1 = lane == i1
        oh2 = lane == i2
        both = jnp.where(jnp.logical_or(oh1, oh2), 1.0, 0.0)
        before = jnp.dot(tri_ref[...], both.astype(BF16), preferred_element_type=F32) + carry
        rank1 = jnp.sum(jnp.where(oh1, before, 0.0), axis=-1, keepdims=True)
        rank2 = jnp.sum(jnp.where(oh2, before, 0.0), axis=-1, keepdims=True)

        out = jnp.where(lane == RT_EID, (i1 - ROUTE_LANE0).astype(F32), 0.0)
        out = jnp.where(lane == RT_EID + 1, (i2 - ROUTE_LANE0).astype(F32), out)
        out = jnp.where(lane == RT_W, w1, out)
        out = jnp.where(lane == RT_W + 1, w2, out)
        out = jnp.where(lane == RT_RANK, rank1, out)
        out = jnp.where(lane == RT_RANK + 1, rank2, out)
        route_ref[rows, :] = out
        return carry + jnp.sum(both, axis=0, keepdims=True)

    carry = carry_ref[0:1, :]
    for c in range(tm // MERGE_ROWS):
        carry = row_chunk(pl.ds(c * MERGE_ROWS, MERGE_ROWS), carry)
    carry_ref[...] = jnp.broadcast_to(carry, carry_ref.shape)
    cnt_ref[...] = jnp.broadcast_to(carry, cnt_ref.shape)


def _merge(x2d, r, os_, lses, proj, lw):
    T = x2d.shape[0]
    tm = MERGE_TILE
    row = lambda w: pl.BlockSpec((tm, w), lambda i: (i, 0))
    pcol = lambda w, cb: pl.BlockSpec((tm, w), lambda i: (i, cb))
    full = lambda a: pl.BlockSpec(a.shape, lambda i: (0,) * a.ndim)
    gate_cb = CB_GATE * COL_BLK // D_MODEL
    weights = [lw["w_ret_out"], lw["w_att_out"], lw["w_sg_out"], lw["w_o"], lw["sg_w"], lw["sg_b_full"],
               lw["sg_ln_gain"], lw["norm_ffn"], lw["w_router"], lw["b_router"], lw["tri"]]
    return pl.pallas_call(
        _merge_kernel,
        grid=(T // tm,),
        in_specs=[row(D_MODEL), row(RET_V_W)] + [row(2 * LANES)] * (2 * len(ATT_PATTERNS)) + [
                  pcol(SG_WIDTH, CB_SGU), pcol(SG_WIDTH, CB_SGV),
                  pcol(D_MODEL, gate_cb), pcol(D_MODEL, gate_cb + 1), pcol(D_MODEL, gate_cb + 2)
                 ] + [full(a) for a in weights],
        out_specs=[row(D_MODEL), row(D_MODEL // 2), row(LANES), pl.BlockSpec((8, LANES), lambda i: (0, 0))],
        out_shape=[jax.ShapeDtypeStruct((T, D_MODEL), F32), jax.ShapeDtypeStruct((T, D_MODEL // 2), U32),
                   jax.ShapeDtypeStruct((T, LANES), F32), jax.ShapeDtypeStruct((8, LANES), F32)],
        scratch_shapes=[pltpu.VMEM((8, LANES), F32)],
        compiler_params=_cparams(("arbitrary",)),
        name="merge",
    )(x2d, r, *os_, *lses, proj, proj, proj, proj, proj, *weights)


DISPATCH_TILE = 1024
ISSUE_UNROLL = 8


def _dispatch_kernel(dest_ref, h_ref, xb_in_ref, xb_ref, sem):
    del xb_in_ref
    tm = DISPATCH_TILE

    def issue(g, carry):
        base = pl.multiple_of(g * ISSUE_UNROLL, ISSUE_UNROLL)
        for u in range(ISSUE_UNROLL):
            src = h_ref.at[pl.ds(base + u, 1)]
            for k in range(TOP_K):
                pltpu.make_async_copy(src, xb_ref.at[pl.ds(dest_ref[0, 0, TOP_K * (base + u) + k], 1)], sem).start()
        return carry

    lax.fori_loop(0, tm // ISSUE_UNROLL, issue, 0)
    pltpu.make_async_copy(xb_ref.at[pl.ds(0, TOP_K * tm)], xb_ref.at[pl.ds(0, TOP_K * tm)], sem).wait()


def _dispatch(hp, dest, P):
    T = hp.shape[0]
    tm = DISPATCH_TILE
    half = D_MODEL // 2
    return pl.pallas_call(
        _dispatch_kernel,
        grid=(T // tm,),
        in_specs=[
            pl.BlockSpec((1, 1, TOP_K * tm), lambda i: (i, 0, 0), memory_space=pltpu.SMEM),
            pl.BlockSpec((tm, half), lambda i: (i, 0)),
            pl.BlockSpec(memory_space=pl.ANY),
        ],
        out_specs=pl.BlockSpec(memory_space=pl.ANY),
        out_shape=jax.ShapeDtypeStruct((P, half), U32),
        scratch_shapes=[pltpu.SemaphoreType.DMA(())],
        input_output_aliases={2: 0},
        compiler_params=_cparams(("arbitrary",)),
        name="dispatch",
    )(dest.reshape(T // tm, 1, TOP_K * tm), hp, jnp.zeros((P, half), U32))


def _experts_kernel(blk_e_ref, nused_ref, x_ref, wg_ref, wu_ref, wd_ref, y_ref, wg_s, wu_s, wd_s):
    i = pl.program_id(0)

    @pl.when(jnp.logical_or(i == 0, blk_e_ref[i] != blk_e_ref[jnp.maximum(i - 1, 0)]))
    def _():
        wg_s[...] = wg_ref[0, 0].astype(BF16)
        wu_s[...] = wu_ref[0, 0].astype(BF16)
        wd_s[...] = wd_ref[0, 0].astype(BF16)

    @pl.when(i < nused_ref[0])
    def _():
        x = jnp.concatenate(_unpack_halves(x_ref[...]), axis=1).astype(BF16)
        gate = jnp.dot(x, wg_s[...], preferred_element_type=F32)
        up = jnp.dot(x, wu_s[...], preferred_element_type=F32)
        hid = (gate * _sigmoid(gate) * up).astype(BF16)
        y_ref[...] = jnp.dot(hid, wd_s[...], preferred_element_type=F32)

    @pl.when(i >= nused_ref[0])
    def _():
        y_ref[...] = jnp.zeros_like(y_ref)


def _experts(xb, blk_e, nused, wg, wu, wd, layer):
    P = xb.shape[0]
    bm = MOE_BLOCK
    grid_spec = pltpu.PrefetchScalarGridSpec(
        num_scalar_prefetch=2,
        grid=(P // bm,),
        in_specs=[
            pl.BlockSpec((bm, D_MODEL // 2), lambda i, e, n: (i, 0)),
            pl.BlockSpec((1, 1, D_MODEL, EXP_HIDDEN), lambda i, e, n: (layer, e[i], 0, 0)),
            pl.BlockSpec((1, 1, D_MODEL, EXP_HIDDEN), lambda i, e, n: (layer, e[i], 0, 0)),
            pl.BlockSpec((1, 1, EXP_HIDDEN, D_MODEL), lambda i, e, n: (layer, e[i], 0, 0)),
        ],
        out_specs=pl.BlockSpec((bm, D_MODEL), lambda i, e, n: (i, 0)),
        scratch_shapes=[pltpu.VMEM((D_MODEL, EXP_HIDDEN), BF16), pltpu.VMEM((D_MODEL, EXP_HIDDEN), BF16),
                        pltpu.VMEM((EXP_HIDDEN, D_MODEL), BF16)],
    )
    return pl.pallas_call(
        _experts_kernel,
        grid_spec=grid_spec,
        out_shape=jax.ShapeDtypeStruct((P, D_MODEL), F32),
        compiler_params=_cparams(("arbitrary",)),
        name="experts",
    )(blk_e, nused, xb, wg, wu, wd)


COMBINE_TILE = 512


def _combine_kernel(dest_ref, dest_next_ref, x_ref, route_ref, gain_ref, y_ref, o_ref, buf_ref, sems, *, final_norm):
    i = pl.program_id(0)
    n = pl.num_programs(0)
    tm = COMBINE_TILE

    def gather(idx_ref, slot):
        def issue(g, carry):
            base = pl.multiple_of(g * ISSUE_UNROLL, ISSUE_UNROLL)
            for u in range(ISSUE_UNROLL):
                for k in range(TOP_K):
                    pltpu.make_async_copy(y_ref.at[pl.ds(idx_ref[0, 0, TOP_K * (base + u) + k], 1)],
                                          buf_ref.at[slot, k, pl.ds(base + u, 1)], sems.at[slot]).start()
            return carry

        lax.fori_loop(0, tm // ISSUE_UNROLL, issue, 0)

    @pl.when(i == 0)
    def _():
        gather(dest_ref, 0)

    def step(slot):
        @pl.when(i + 1 < n)
        def _():
            gather(dest_next_ref, 1 - slot)

        pltpu.make_async_copy(buf_ref.at[slot], buf_ref.at[slot], sems.at[slot]).wait()
        route = route_ref[...]
        x = x_ref[...] + sum(route[:, RT_W + k:RT_W + k + 1] * buf_ref[slot, k] for k in range(TOP_K))
        if final_norm:
            ms = jnp.mean(x * x, axis=-1, keepdims=True)
            x = x * lax.rsqrt(ms + NORM_EPS) * gain_ref[...]
        o_ref[...] = x

    for slot in range(2):
        pl.when(i % 2 == slot)(functools.partial(step, slot))


def _combine(x2, route, yb, dest, gain, final_norm):
    T = x2.shape[0]
    tm = COMBINE_TILE
    nt = T // tm
    dest3 = dest.reshape(nt, 1, TOP_K * tm)
    return pl.pallas_call(
        functools.partial(_combine_kernel, final_norm=final_norm),
        grid=(nt,),
        in_specs=[
            pl.BlockSpec((1, 1, TOP_K * tm), lambda i: (i, 0, 0), memory_space=pltpu.SMEM),
            pl.BlockSpec((1, 1, TOP_K * tm), lambda i: (jnp.minimum(i + 1, nt - 1), 0, 0), memory_space=pltpu.SMEM),
            pl.BlockSpec((tm, D_MODEL), lambda i: (i, 0)),
            pl.BlockSpec((tm, LANES), lambda i: (i, 0)),
            pl.BlockSpec((1, D_MODEL), lambda i: (0, 0)),
            pl.BlockSpec(memory_space=pl.ANY),
        ],
        out_specs=pl.BlockSpec((tm, D_MODEL), lambda i: (i, 0)),
        out_shape=jax.ShapeDtypeStruct((T, D_MODEL), F32),
        scratch_shapes=[pltpu.VMEM((2, TOP_K, tm, D_MODEL), F32), pltpu.SemaphoreType.DMA((2,))],
        compiler_params=_cparams(("arbitrary",)),
        name="combine_final" if final_norm else "combine",
    )(dest3, dest3, x2, route, gain.reshape(1, D_MODEL), yb)


def _layer_weights(l, p):
    w_router = jnp.zeros((D_MODEL, LANES), F32)
    w_router = w_router.at[:, :N_GROUPS].set(p["w_router_group"][l])
    w_router = w_router.at[:, ROUTE_LANE0:ROUTE_LANE0 + N_EXPERTS].set(p["w_router_expert"][l])
    b_router = jnp.zeros((1, LANES), F32)
    b_router = b_router.at[0, :N_GROUPS].set(p["b_router_group"][l])
    b_router = b_router.at[0, ROUTE_LANE0:ROUTE_LANE0 + N_EXPERTS].set(p["b_router_expert"][l])
    tri = (jnp.arange(MERGE_ROWS)[:, None] > jnp.arange(MERGE_ROWS)[None, :]).astype(BF16)
    gw = SG_WIDTH // SG_GROUPS
    w_in, w_in_rot = _in_proj_weights(p["w_in"][l])
    return {
        "norm_mix": p["norm_mix"][l],
        "w_in": w_in,
        "w_in_rot": w_in_rot,
        "lg": jnp.stack([-jnp.exp(p["ret_decay_fwd"][l].astype(F32)), -jnp.exp(p["ret_decay_bwd"][l].astype(F32))]),
        "ret_gn_gain": p["ret_gn_gain"][l],
        "w_ret_out": p["w_ret_out"][l].astype(BF16),
        "w_att_out": p["w_att_out"][l].astype(BF16),
        "w_sg_out": p["w_sg_out"][l].astype(BF16),
        "w_o": p["w_o"][l].astype(BF16),
        "sg_w": p["sg_w"][l].astype(BF16),
        "sg_b_full": jnp.repeat(p["sg_b"][l].T, gw, axis=1),
        "sg_ln_gain": p["sg_ln_gain"][l].reshape(1, SG_WIDTH),
        "norm_ffn": p["norm_ffn"][l].reshape(1, D_MODEL),
        "w_router": jnp.concatenate([w_router.astype(BF16),
                                     (w_router - w_router.astype(BF16).astype(F32)).astype(BF16)], axis=1),
        "b_router": b_router,
        "tri": tri,
    }


def _moe_plan(route, cnt, T):
    bm = MOE_BLOCK
    eid = route[:, RT_EID:RT_EID + TOP_K].astype(I32)
    rank = route[:, RT_RANK:RT_RANK + TOP_K].astype(I32)
    counts = cnt[0, ROUTE_LANE0:ROUTE_LANE0 + N_EXPERTS].astype(I32)
    padded = (counts + bm - 1) // bm * bm
    pad_end = jnp.cumsum(padded)
    pad_start = pad_end - padded
    experts = jnp.arange(N_EXPERTS, dtype=I32)
    dest = jnp.sum(jnp.where(eid[..., None] == experts, pad_start, 0), axis=-1) + rank
    P = T * TOP_K + N_EXPERTS * bm
    nblk = P // bm
    blk_start = jnp.arange(nblk, dtype=I32) * bm
    blk_e = jnp.minimum(jnp.sum((pad_end[None, :] <= blk_start[:, None]).astype(I32), axis=1), N_EXPERTS - 1)
    nused = (pad_end[-1:] // bm).astype(I32)
    return dest.reshape(-1), blk_e, nused, P


def _run_trunk(x, lws, w_exp, tabs, norm_final):
    B, S, _ = x.shape
    T = B * S
    x2d = x.reshape(T, D_MODEL)
    tm_in = min(2048, S)
    for l, lw in enumerate(lws):
        proj, attp = _in_proj(x2d, lw["norm_mix"], lw["w_in"], lw["w_in_rot"], tabs, S, tm_in)
        r = _retention(proj, lw["lg"], lw["ret_gn_gain"], B, S)
        os_, lses = zip(*[_dil_attn(attp, g, B, S) for g in range(len(ATT_PATTERNS))])
        x2, hp, route, cnt = _merge(x2d, r, os_, lses, proj, lw)
        dest, blk_e, nused, P = _moe_plan(route, cnt, T)
        xb = _dispatch(hp, dest, P)
        yb = _experts(xb, blk_e, nused, *w_exp, layer=l)
        x2d = _combine(x2, route, yb, dest, norm_final, final_norm=(l == len(lws) - 1))
    return x2d.reshape(B, S, D_MODEL)


def kernel(x_prompt, x_sample, norm_mix, w_in, ret_decay_fwd, ret_decay_bwd, ret_gn_gain, sg_ln_gain, sg_w, sg_b,
           w_ret_out, w_att_out, w_sg_out, w_o, norm_ffn, w_router_group, b_router_group, w_router_expert,
           b_router_expert, w_exp_gate, w_exp_up, w_exp_down, norm_final):
    p = dict(norm_mix=norm_mix, w_in=w_in, ret_decay_fwd=ret_decay_fwd, ret_decay_bwd=ret_decay_bwd,
             ret_gn_gain=ret_gn_gain, sg_ln_gain=sg_ln_gain, sg_w=sg_w, sg_b=sg_b, w_ret_out=w_ret_out,
             w_att_out=w_att_out, w_sg_out=w_sg_out, w_o=w_o, norm_ffn=norm_ffn, w_router_group=w_router_group,
             b_router_group=b_router_group, w_router_expert=w_router_expert, b_router_expert=b_router_expert)
    lws = [_layer_weights(l, p) for l in range(DEPTH)]
    s_max = max(x_prompt.shape[1], x_sample.shape[1])
    tabs = _ret_rot_tables(s_max) + _att_rot_tables(s_max)
    w_exp = (w_exp_gate, w_exp_up, w_exp_down)
    y_prompt = _run_trunk(x_prompt, lws, w_exp, tabs, norm_final)
    y_sample = _run_trunk(x_sample, lws, w_exp, tabs, norm_final)
    return (y_prompt, y_sample)
```
